```python
import jax
import jax.numpy as jnp
from jax import lax
import numpy as np

D_MODEL = 4096
BATCH = 2
SEQ = 8192
DEPTH = 2

N_MEM = 256
HEAD_DIM = 128
N_HEADS_ATTN = D_MODEL // HEAD_DIM
N_HEADS_SB = N_HEADS_ATTN // 2
N_HEADS_FOX = N_HEADS_ATTN - N_HEADS_SB
ATTN_WIDTH = N_HEADS_ATTN * HEAD_DIM
Q_BLOCK = 128
GLA_HEADS = 8
GLA_DK = D_MODEL // 2 // GLA_HEADS
GLA_DV = D_MODEL // GLA_HEADS
GLA_GATE_RANK = 16
GLA_TAU = 16.0
GLA_CHUNK = 64
MEM_HEADS = 4
MEM_HEAD_DIM = 128
D_FF_DENSE = 11008
N_EXPERTS = 8
TOP_K = 2
D_FF_EXPERT = 6144
MOE_BLOCK = 256
LN_EPS = 1e-5
RMS_EPS = 1e-6
DEEPNORM_ALPHA = (2.0 * DEPTH) ** 0.25
DEEPNORM_BETA = (8.0 * DEPTH) ** -0.25

kernel_name = 'hybrid_sb_fox_gla_moe_deepnorm'


def layer_norm(x, g, b):
    xf = x.astype(jnp.float32)
    mu = jnp.mean(xf, axis=-1, keepdims=True)
    var = jnp.mean(jnp.square(xf - mu), axis=-1, keepdims=True)
    y = (xf - mu) * lax.rsqrt(var + LN_EPS) * g.astype(jnp.float32) + b.astype(jnp.float32)
    return y.astype(x.dtype)


def post_norm(x, h, g, b):
    return layer_norm(DEEPNORM_ALPHA * x + h, g, b)


def split_heads(t, n_heads):
    B, S, _ = t.shape
    return t.reshape(B, S, n_heads, -1).transpose(0, 2, 1, 3)


def merge_heads(t):
    B, H, S, d = t.shape
    return t.transpose(0, 2, 1, 3).reshape(B, S, H * d)


def stick_breaking_block(q_blk, k, v, q_start):
    d = q_blk.shape[-1]
    z = jnp.einsum('bhqd,bhkd->bhqk', q_blk, k).astype(jnp.float32) * (d ** -0.5)
    qi = q_start + jnp.arange(q_blk.shape[2])[:, None]
    kj = jnp.arange(k.shape[2])[None, :]
    before = kj < qi
    sp = jnp.where(before, jax.nn.softplus(z), 0.0)
    rem = lax.cumsum(sp, axis=3, reverse=True)
    w = jnp.exp(jnp.where(before, z - rem, -jnp.inf))
    return jnp.einsum('bhqk,bhkd->bhqd', w.astype(v.dtype), v)


def forgetting_block(q_blk, k, v, f_q, f_k, q_start):
    d = q_blk.shape[-1]
    z = jnp.einsum('bhqd,bhkd->bhqk', q_blk, k).astype(jnp.float32) * (d ** -0.5)
    z = z + f_q[..., :, None] - f_k[..., None, :]
    qi = q_start + jnp.arange(q_blk.shape[2])[:, None]
    kj = jnp.arange(k.shape[2])[None, :]
    z = jnp.where(kj <= qi, z, -jnp.inf)
    p = jax.nn.softmax(z, axis=-1)
    return jnp.einsum('bhqk,bhkd->bhqd', p.astype(v.dtype), v)


def sb_fox_mixer(x, w_in, b_forget, w_out):
    B, S, D = x.shape
    W = ATTN_WIDTH
    proj = x @ w_in
    q = split_heads(proj[..., :W], N_HEADS_ATTN)
    k = split_heads(proj[..., W:2 * W], N_HEADS_ATTN)
    v = split_heads(proj[..., 2 * W:3 * W], N_HEADS_ATTN)
    log_f = jax.nn.log_sigmoid((proj[..., 3 * W:] + b_forget).astype(jnp.float32))
    F = jnp.cumsum(log_f, axis=1).transpose(0, 2, 1)
    qa, ka, va = q[:, :N_HEADS_SB], k[:, :N_HEADS_SB], v[:, :N_HEADS_SB]
    qb, kb, vb = q[:, N_HEADS_SB:], k[:, N_HEADS_SB:], v[:, N_HEADS_SB:]
    outs_a, outs_b = [], []
    for start in range(0, S, Q_BLOCK):
        end = start + Q_BLOCK
        outs_a.append(stick_breaking_block(qa[:, :, start:end], ka[:, :, :end], va[:, :, :end], start))
        outs_b.append(forgetting_block(qb[:, :, start:end], kb[:, :, :end], vb[:, :, :end],
                                       F[:, :, start:end], F[:, :, :end], start))
    o = jnp.concatenate([jnp.concatenate(outs_a, axis=2), jnp.concatenate(outs_b, axis=2)], axis=1)
    return merge_heads(o) @ w_out


def gla_mixer(x, w_in, w_gate_lr, b_gate, g_head_norm, w_out):
    B, S, D = x.shape
    f32 = jnp.float32
    dk_tot = GLA_HEADS * GLA_DK
    dv_tot = GLA_HEADS * GLA_DV
    proj = x @ w_in
    q, k, v, r, g_low = jnp.split(proj, [dk_tot, 2 * dk_tot, 2 * dk_tot + dv_tot, 2 * dk_tot + 2 * dv_tot], axis=-1)
    log_a = jax.nn.log_sigmoid((g_low @ w_gate_lr + b_gate).astype(f32)) / GLA_TAU
    C = GLA_CHUNK
    N = S // C

    def chunk(t, d):
        return t.reshape(B, N, C, GLA_HEADS, d).transpose(0, 3, 1, 2, 4)

    qc = chunk(q.astype(f32), GLA_DK) * (GLA_DK ** -0.5)
    kc = chunk(k.astype(f32), GLA_DK)
    vc = chunk(v.astype(f32), GLA_DV)
    G = jnp.cumsum(chunk(log_a, GLA_DK), axis=3)
    G_last = G[:, :, :, -1:, :]
    q_dec = qc * jnp.exp(G)
    k_inv = kc * jnp.exp(-G)
    k_tail = kc * jnp.exp(G_last - G)
    causal = jnp.tril(jnp.ones((C, C), dtype=bool))
    attn = jnp.where(causal, jnp.einsum('bhnck,bhnsk->bhncs', q_dec, k_inv), 0.0)
    o_intra = jnp.einsum('bhncs,bhnsv->bhncv', attn, vc)

    def step(state, inp):
        q_n, k_n, v_n, decay_n = inp
        o_n = jnp.einsum('bhck,bhkv->bhcv', q_n, state)
        state = decay_n[..., None] * state + jnp.einsum('bhck,bhcv->bhkv', k_n, v_n)
        return state, o_n

    state0 = jnp.zeros((B, GLA_HEADS, GLA_DK, GLA_DV), f32)
    xs = (jnp.moveaxis(q_dec, 2, 0), jnp.moveaxis(k_tail, 2, 0), jnp.moveaxis(vc, 2, 0),
          jnp.moveaxis(jnp.exp(G[:, :, :, -1, :]), 2, 0))
    _, o_inter = lax.scan(step, state0, xs)
    o = o_intra + jnp.moveaxis(o_inter, 0, 2)
    o = o * lax.rsqrt(jnp.mean(jnp.square(o), axis=-1, keepdims=True) + RMS_EPS)
    o = o.transpose(0, 2, 3, 1, 4).reshape(B, S, dv_tot) * g_head_norm.astype(f32)
    o = o.astype(x.dtype) * jax.nn.silu(r)
    return o @ w_out


def memory_cross_attention(x, mem, w_q, w_kv, w_o):
    q = split_heads(x @ w_q, MEM_HEADS)
    k, v = jnp.split(mem @ w_kv, 2, axis=-1)
    k = split_heads(k, MEM_HEADS)
    v = split_heads(v, MEM_HEADS)
    s = jnp.einsum('bhqd,bhmd->bhqm', q, k).astype(jnp.float32) * (MEM_HEAD_DIM ** -0.5)
    p = jax.nn.softmax(s, axis=-1)
    o = jnp.einsum('bhqm,bhmd->bhqd', p.astype(v.dtype), v)
    return merge_heads(o) @ w_o


def swiglu(x, w_gate_up, w_down):
    gate, up = jnp.split(x @ w_gate_up, 2, axis=-1)
    return (jax.nn.silu(gate) * up) @ w_down


def moe_swiglu(x, w_router, w_gate_up, w_down):
    B, S, D = x.shape
    T = B * S
    xt = x.reshape(T, D)
    logits = (xt @ w_router).astype(jnp.float32)
    top_logit, top_e = lax.top_k(logits, TOP_K)
    gate = jax.nn.softmax(top_logit, axis=-1)
    flat_e = top_e.reshape(-1)
    flat_tok = jnp.repeat(jnp.arange(T, dtype=jnp.int32), TOP_K)
    flat_g = gate.reshape(-1)
    order = jnp.argsort(flat_e)
    se, stok, sg = flat_e[order], flat_tok[order], flat_g[order]
    counts = jnp.zeros((N_EXPERTS,), jnp.int32).at[flat_e].add(1)
    padded = (counts + MOE_BLOCK - 1) // MOE_BLOCK * MOE_BLOCK
    start = jnp.cumsum(counts) - counts
    pend = jnp.cumsum(padded)
    pstart = pend - padded
    dest = pstart[se] + jnp.arange(T * TOP_K, dtype=jnp.int32) - start[se]
    n_blocks = -(-(T * TOP_K) // MOE_BLOCK) + N_EXPERTS
    rows = jnp.zeros((n_blocks * MOE_BLOCK, D), x.dtype).at[dest].set(xt[stok])
    block_first = jnp.arange(n_blocks, dtype=jnp.int32) * MOE_BLOCK
    block_e = jnp.minimum(jnp.sum(block_first[:, None] >= pend[None, :], axis=1), N_EXPERTS - 1)

    def expert_block(args):
        xb, e = args
        g, u = jnp.split(xb @ w_gate_up[e], 2, axis=-1)
        return (jax.nn.silu(g) * u) @ w_down[e]

    y_rows = lax.map(expert_block, (rows.reshape(n_blocks, MOE_BLOCK, D), block_e)).reshape(-1, D)
    y = y_rows[dest] * sg[:, None].astype(x.dtype)
    out = jnp.zeros((T, D), x.dtype).at[stok].add(y)
    return out.reshape(B, S, D)


def _w(key, shape, fan_in, scale=1.0):
    return jax.random.normal(key, shape, jnp.float32) * (scale * fan_in ** -0.5)


def _gain(key, n):
    return 1.0 + 0.02 * jax.random.normal(key, (n,), jnp.float32)


def _bias(key, n, scale=0.02):
    return scale * jax.random.normal(key, (n,), jnp.float32)


def setup_inputs(seed: int = 0) -> dict:
    key = jax.random.key(seed)
    ks = iter(jax.random.split(key, 40))
    D = D_MODEL
    mem_w = MEM_HEADS * MEM_HEAD_DIM
    gla_in = 2 * GLA_HEADS * GLA_DK + 2 * GLA_HEADS * GLA_DV + GLA_GATE_RANK
    beta = DEEPNORM_BETA
    inp = {}
    inp['x'] = jax.random.normal(next(ks), (BATCH, SEQ, D), jnp.float32)
    inp['mem'] = jax.random.normal(next(ks), (BATCH, N_MEM, D), jnp.float32)
    inp['l0_w_in'] = _w(next(ks), (D, 3 * ATTN_WIDTH + N_HEADS_FOX), D)
    inp['l0_b_forget'] = _bias(next(ks), N_HEADS_FOX, 0.1)
    inp['l0_w_out'] = _w(next(ks), (ATTN_WIDTH, D), ATTN_WIDTH, beta)
    inp['l0_ln_mix_g'] = _gain(next(ks), D)
    inp['l0_ln_mix_b'] = _bias(next(ks), D)
    inp['l0_mem_wq'] = _w(next(ks), (D, mem_w), D)
    inp['l0_mem_wkv'] = _w(next(ks), (D, 2 * mem_w), D)
    inp['l0_mem_wo'] = _w(next(ks), (mem_w, D), mem_w, beta)
    inp['l0_ln_mem_g'] = _gain(next(ks), D)
    inp['l0_ln_mem_b'] = _bias(next(ks), D)
    inp['l0_ffn_w_gate_up'] = _w(next(ks), (D, 2 * D_FF_DENSE), D)
    inp['l0_ffn_w_down'] = _w(next(ks), (D_FF_DENSE, D), D_FF_DENSE, beta)
    inp['l0_ln_ffn_g'] = _gain(next(ks), D)
    inp['l0_ln_ffn_b'] = _bias(next(ks), D)
    inp['l1_w_in'] = _w(next(ks), (D, gla_in), D)
    inp['l1_w_gate_lr'] = _w(next(ks), (GLA_GATE_RANK, GLA_HEADS * GLA_DK), GLA_GATE_RANK)
    inp['l1_b_gate'] = _bias(next(ks), GLA_HEADS * GLA_DK, 0.1)
    inp['l1_g_head_norm'] = _gain(next(ks), GLA_HEADS * GLA_DV)
    inp['l1_w_out'] = _w(next(ks), (GLA_HEADS * GLA_DV, D), GLA_HEADS * GLA_DV, beta)
    inp['l1_ln_mix_g'] = _gain(next(ks), D)
    inp['l1_ln_mix_b'] = _bias(next(ks), D)
    inp['l1_mem_wq'] = _w(next(ks), (D, mem_w), D)
    inp['l1_mem_wkv'] = _w(next(ks), (D, 2 * mem_w), D)
    inp['l1_mem_wo'] = _w(next(ks), (mem_w, D), mem_w, beta)
    inp['l1_ln_mem_g'] = _gain(next(ks), D)
    inp['l1_ln_mem_b'] = _bias(next(ks), D)
    inp['l1_router'] = _w(next(ks), (D, N_EXPERTS), D)
    inp['l1_moe_w_gate_up'] = _w(next(ks), (N_EXPERTS, D, 2 * D_FF_EXPERT), D)
    inp['l1_moe_w_down'] = _w(next(ks), (N_EXPERTS, D_FF_EXPERT, D), D_FF_EXPERT, beta)
    inp['l1_ln_ffn_g'] = _gain(next(ks), D)
    inp['l1_ln_ffn_b'] = _bias(next(ks), D)
    return inp


def reference(x, mem,
              l0_w_in, l0_b_forget, l0_w_out, l0_ln_mix_g, l0_ln_mix_b,
              l0_mem_wq, l0_mem_wkv, l0_mem_wo, l0_ln_mem_g, l0_ln_mem_b,
              l0_ffn_w_gate_up, l0_ffn_w_down, l0_ln_ffn_g, l0_ln_ffn_b,
              l1_w_in, l1_w_gate_lr, l1_b_gate, l1_g_head_norm, l1_w_out, l1_ln_mix_g, l1_ln_mix_b,
              l1_mem_wq, l1_mem_wkv, l1_mem_wo, l1_ln_mem_g, l1_ln_mem_b,
              l1_router, l1_moe_w_gate_up, l1_moe_w_down, l1_ln_ffn_g, l1_ln_ffn_b):
    layers = (
        ((l0_w_in, l0_b_forget, l0_w_out), (l0_ln_mix_g, l0_ln_mix_b),
         (l0_mem_wq, l0_mem_wkv, l0_mem_wo), (l0_ln_mem_g, l0_ln_mem_b),
         (l0_ffn_w_gate_up, l0_ffn_w_down), (l0_ln_ffn_g, l0_ln_ffn_b)),
        ((l1_w_in, l1_w_gate_lr, l1_b_gate, l1_g_head_norm, l1_w_out), (l1_ln_mix_g, l1_ln_mix_b),
         (l1_mem_wq, l1_mem_wkv, l1_mem_wo), (l1_ln_mem_g, l1_ln_mem_b),
         (l1_router, l1_moe_w_gate_up, l1_moe_w_down), (l1_ln_ffn_g, l1_ln_ffn_b)),
    )
    for layer in range(DEPTH):
        mix_p, ln_mix, mem_p, ln_mem, ffn_p, ln_ffn = layers[layer]
        if layer % 2 == 0:
            h = sb_fox_mixer(x, *mix_p)
        else:
            h = gla_mixer(x, *mix_p)
        x = post_norm(x, h, *ln_mix)
        x = post_norm(x, memory_cross_attention(x, mem, *mem_p), *ln_mem)
        if layer % 2 == 0:
            h = swiglu(x, *ffn_p)
        else:
            h = moe_swiglu(x, *ffn_p)
        x = post_norm(x, h, *ln_ffn)
    return x
```

```python
import functools

import jax
import jax.numpy as jnp
from jax import lax
from jax.experimental import pallas as pl
from jax.experimental.pallas import tpu as pltpu

HEAD_DIM = 128
MEM_HEADS = 4
MEM_HEAD_DIM = 128
GLA_DK = 256
GLA_DV = 512
GLA_CHUNK = 64
GLA_TAU = 16.0
TOP_K = 2
LN_EPS = 1e-5
RMS_EPS = 1e-6
DEPTH = 2
DEEPNORM_ALPHA = (2.0 * DEPTH) ** 0.25
NEG_BIG = -1e30
LANE = 128
MIB = 1024 * 1024

_HI = lax.Precision.HIGHEST
_NT = (((1,), (1,)), ((), ()))
_TN = (((0,), (0,)), ((), ()))


def _params(sem, vmem_mib):
    return pltpu.CompilerParams(dimension_semantics=sem, vmem_limit_bytes=vmem_mib * MIB)


def _softplus(z):
    return jnp.maximum(z, 0.0) + jnp.log(1.0 + jnp.exp(-jnp.abs(z)))


def _log_sigmoid(z):
    return jnp.minimum(z, 0.0) - jnp.log(1.0 + jnp.exp(-jnp.abs(z)))


def _silu(z):
    return z / (1.0 + jnp.exp(-z))


def _mm_kernel(x_ref, w_ref, o_ref):
    o_ref[...] = jnp.dot(x_ref[...], w_ref[...],
                         preferred_element_type=jnp.float32).astype(o_ref.dtype)


def _matmul(x, w, out_dtype, tm=1024, tn=1024, name="matmul"):
    M, K = x.shape
    N = w.shape[1]
    tm, tn = min(tm, M), min(tn, N)
    assert M % tm == 0 and N % tn == 0
    return pl.pallas_call(
        _mm_kernel,
        grid=(M // tm, N // tn),
        in_specs=[pl.BlockSpec((tm, K), lambda i, j: (i, 0)),
                  pl.BlockSpec((K, tn), lambda i, j: (0, j))],
        out_specs=pl.BlockSpec((tm, tn), lambda i, j: (i, j)),
        out_shape=jax.ShapeDtypeStruct((M, N), out_dtype),
        compiler_params=_params(("parallel", "parallel"), 56),
        name=name,
    )(x, w)


def _mm_hi_kernel(x_ref, w_ref, o_ref):
    o_ref[...] = jnp.dot(x_ref[...], w_ref[...], precision=_HI,
                         preferred_element_type=jnp.float32)


def _matmul_f32_narrow(x, w, tm=512, name="narrow_proj"):
    M, K = x.shape
    N = w.shape[1]
    tm = min(tm, M)
    assert M % tm == 0
    return pl.pallas_call(
        _mm_hi_kernel,
        grid=(M // tm,),
        in_specs=[pl.BlockSpec((tm, K), lambda i: (i, 0)),
                  pl.BlockSpec((K, N), lambda i: (0, 0))],
        out_specs=pl.BlockSpec((tm, N), lambda i: (i, 0)),
        out_shape=jax.ShapeDtypeStruct((M, N), jnp.float32),
        compiler_params=_params(("parallel",), 48),
        name=name,
    )(x, w)


def _layer_norm_rows(y, g, b):
    mu = jnp.mean(y, axis=-1, keepdims=True)
    yc = y - mu
    var = jnp.mean(yc * yc, axis=-1, keepdims=True)
    return yc * lax.rsqrt(var + LN_EPS) * g + b


def _add_ln_kernel(x_ref, h_ref, g_ref, b_ref, o_ref, ob_ref):
    y = DEEPNORM_ALPHA * x_ref[...] + h_ref[...].astype(jnp.float32)
    out = _layer_norm_rows(y, g_ref[...], b_ref[...])
    o_ref[...] = out
    ob_ref[...] = out.astype(jnp.bfloat16)


def _add_ln(x, h, g, b, tm=256, name="add_ln"):
    T, D = x.shape
    tm = min(tm, T)
    row = pl.BlockSpec((tm, D), lambda i: (i, 0))
    vec = pl.BlockSpec((1, D), lambda i: (0, 0))
    return pl.pallas_call(
        _add_ln_kernel,
        grid=(T // tm,),
        in_specs=[row, row, vec, vec],
        out_specs=[row, row],
        out_shape=[jax.ShapeDtypeStruct((T, D), jnp.float32),
                   jax.ShapeDtypeStruct((T, D), jnp.bfloat16)],
        compiler_params=_params(("parallel",), 48),
        name=name,
    )(x, h, g.reshape(1, D), b.reshape(1, D))


def _fgate_kernel(x_ref, wf_ref, b_ref, f_ref, carry_ref):
    s = pl.program_id(1)

    @pl.when(s == 0)
    def _():
        carry_ref[...] = jnp.zeros_like(carry_ref)

    ts = x_ref.shape[0]
    p = lax.dot_general(wf_ref[...], x_ref[...], _NT, precision=_HI,
                        preferred_element_type=jnp.float32)
    lf = _log_sigmoid(p + b_ref[...])
    r = lax.broadcasted_iota(jnp.int32, (ts, ts), 0)
    c = lax.broadcasted_iota(jnp.int32, (ts, ts), 1)
    upper = jnp.where(r <= c, 1.0, 0.0).astype(jnp.float32)
    cs = jnp.dot(lf, upper, precision=_HI, preferred_element_type=jnp.float32)
    cs = cs + carry_ref[:, 0:1]
    f_ref[0] = cs
    carry_ref[...] = jnp.broadcast_to(cs[:, ts - 1:ts], carry_ref.shape)


def _forget_prefix(x, wf_t, b_forget, batch, ts=512):
    T, D = x.shape
    S = T // batch
    Hf = wf_t.shape[0]
    ts = min(ts, S)
    ns = S // ts
    return pl.pallas_call(
        _fgate_kernel,
        grid=(batch, ns),
        in_specs=[pl.BlockSpec((ts, D), lambda b, s: (b * ns + s, 0)),
                  pl.BlockSpec((Hf, D), lambda b, s: (0, 0)),
                  pl.BlockSpec((Hf, 1), lambda b, s: (0, 0))],
        out_specs=pl.BlockSpec((1, Hf, ts), lambda b, s: (b, 0, s)),
        out_shape=jax.ShapeDtypeStruct((batch, Hf, S), jnp.float32),
        scratch_shapes=[pltpu.VMEM((Hf, LANE), jnp.float32)],
        compiler_params=_params(("parallel", "arbitrary"), 48),
        name="forget_prefix",
    )(x, wf_t, b_forget.reshape(Hf, 1))


def _attn_kernel(q_ref, k_ref, v_ref, f_ref, o_ref, *, n_sb, tq, scale):
    h = pl.program_id(1)
    i = pl.program_id(2)
    q = q_ref[...]
    row = lax.broadcasted_iota(jnp.int32, (tq, tq), 0)
    col = lax.broadcasted_iota(jnp.int32, (tq, tq), 1)

    def kv_block(j):
        start = pl.multiple_of(j * tq, tq)
        return k_ref[pl.ds(start, tq), :], v_ref[pl.ds(start, tq), :]

    def scores(kb):
        return lax.dot_general(q, kb, _NT, preferred_element_type=jnp.float32) * scale

    @pl.when(h < n_sb)
    def _stick_breaking():
        lower = jnp.where(row >= col, 1.0, 0.0).astype(jnp.bfloat16)

        def block(j, carry, acc, diag):
            kb, vb = kv_block(j)
            z = scores(kb)
            sp = _softplus(z)
            if diag:
                sp = jnp.where(col < row, sp, 0.0)
            hi = sp.astype(jnp.bfloat16)
            lo = (sp - hi.astype(jnp.float32)).astype(jnp.bfloat16)
            rem_local = (jnp.dot(hi, lower, preferred_element_type=jnp.float32)
                         + jnp.dot(lo, lower, preferred_element_type=jnp.float32))
            w = jnp.exp(z - (rem_local + carry))
            if diag:
                w = jnp.where(col < row, w, 0.0)
            acc = acc + jnp.dot(w.astype(jnp.bfloat16), vb, preferred_element_type=jnp.float32)
            return carry + rem_local[:, 0:1], acc

        carry, acc = block(i, jnp.zeros((tq, 1), jnp.float32),
                           jnp.zeros((tq, HEAD_DIM), jnp.float32), True)

        def body(t, st):
            return block(i - 1 - t, st[0], st[1], False)

        carry, acc = lax.fori_loop(0, i, body, (carry, acc))
        o_ref[...] = acc.astype(o_ref.dtype)

    @pl.when(h >= n_sb)
    def _forgetting():
        q_start = pl.multiple_of(i * tq, tq)
        f_ref_val = f_ref[0, :, pl.ds(q_start, tq)][:, 0:1]

        def block(j, m, l, acc, diag):
            kb, vb = kv_block(j)
            start = pl.multiple_of(j * tq, tq)
            s = scores(kb) + (f_ref_val - f_ref[0, :, pl.ds(start, tq)])
            if diag:
                s = jnp.where(col <= row, s, NEG_BIG)
            m_new = jnp.maximum(m, jnp.max(s, axis=-1, keepdims=True))
            a = jnp.exp(m - m_new)
            p = jnp.exp(s - m_new)
            l = a * l + jnp.sum(p, axis=-1, keepdims=True)
            acc = a * acc + jnp.dot(p.astype(jnp.bfloat16), vb, preferred_element_type=jnp.float32)
            return m_new, l, acc

        m, l, acc = block(i, jnp.full((tq, 1), NEG_BIG, jnp.float32),
                          jnp.zeros((tq, 1), jnp.float32),
                          jnp.zeros((tq, HEAD_DIM), jnp.float32), True)

        def body(t, st):
            return block(i - 1 - t, st[0], st[1], st[2], False)

        m, l, acc = lax.fori_loop(0, i, body, (m, l, acc))
        o_ref[...] = (acc / l).astype(o_ref.dtype)


def _sb_fox_attention(qkv, f_prefix, batch, n_heads, n_sb, tq=256):
    T = qkv.shape[0]
    S = T // batch
    tq = min(tq, S)
    nq = S // tq
    n_fox = n_heads - n_sb
    kern = functools.partial(_attn_kernel, n_sb=n_sb, tq=tq, scale=HEAD_DIM ** -0.5)
    return pl.pallas_call(
        kern,
        grid=(batch, n_heads, nq),
        in_specs=[pl.BlockSpec((tq, HEAD_DIM), lambda b, h, i: (b * nq + i, h)),
                  pl.BlockSpec((S, HEAD_DIM), lambda b, h, i: (b, n_heads + h)),
                  pl.BlockSpec((S, HEAD_DIM), lambda b, h, i: (b, 2 * n_heads + h)),
                  pl.BlockSpec((1, 1, S),
                               lambda b, h, i: (b * n_fox + jnp.maximum(h - n_sb, 0), 0, 0))],
        out_specs=pl.BlockSpec((tq, HEAD_DIM), lambda b, h, i: (b * nq + i, h)),
        out_shape=jax.ShapeDtypeStruct((T, n_heads * HEAD_DIM), jnp.bfloat16),
        compiler_params=_params(("parallel", "parallel", "arbitrary"), 48),
        name="sb_fox_attention",
    )(qkv, qkv, qkv, f_prefix)


def _gla_kernel(q_ref, k_ref, v_ref, r_ref, gl_ref, wlr_ref, bg_ref, gn_ref, o_ref, state_ref,
                *, n_chunks):
    C = GLA_CHUNK

    @pl.when(pl.program_id(2) == 0)
    def _():
        state_ref[...] = jnp.zeros_like(state_ref)

    row = lax.broadcasted_iota(jnp.int32, (C, C), 0)
    col = lax.broadcasted_iota(jnp.int32, (C, C), 1)
    causal = row >= col
    tri = jnp.where(causal, 1.0, 0.0).astype(jnp.float32)
    wlr = wlr_ref[...]
    bg = bg_ref[...]
    gn = gn_ref[...]
    scale = GLA_DK ** -0.5

    for c in range(n_chunks):
        rows = pl.ds(c * C, C)
        gate_in = jnp.dot(gl_ref[rows, :], wlr, precision=_HI,
                          preferred_element_type=jnp.float32) + bg
        log_a = _log_sigmoid(gate_in) / GLA_TAU
        G = jnp.dot(tri, log_a, precision=_HI, preferred_element_type=jnp.float32)
        g_last = G[C - 1:C, :]
        qf = q_ref[rows, :].astype(jnp.float32) * scale
        kf = k_ref[rows, :].astype(jnp.float32)
        vb = v_ref[rows, :]
        q_dec = (qf * jnp.exp(G)).astype(jnp.bfloat16)
        k_inv = (kf * jnp.exp(-G)).astype(jnp.bfloat16)
        k_tail = (kf * jnp.exp(g_last - G)).astype(jnp.bfloat16)
        attn = lax.dot_general(q_dec, k_inv, _NT, preferred_element_type=jnp.float32)
        attn = jnp.where(causal, attn, 0.0).astype(jnp.bfloat16)
        o = jnp.dot(attn, vb, preferred_element_type=jnp.float32)
        state_t = state_ref[...]
        o = o + lax.dot_general(q_dec, state_t.astype(jnp.bfloat16), _NT,
                                preferred_element_type=jnp.float32)
        state_ref[...] = state_t * jnp.exp(g_last) + lax.dot_general(
            vb, k_tail, _TN, preferred_element_type=jnp.float32)
        o = o * lax.rsqrt(jnp.mean(o * o, axis=-1, keepdims=True) + RMS_EPS) * gn
        rf = r_ref[rows, :].astype(jnp.float32)
        o_ref[rows, :] = (o * _silu(rf)).astype(o_ref.dtype)


def _gla_core(proj, g_low, wlr, b_gate, g_norm, batch, n_heads, tt=256):
    T = proj.shape[0]
    S = T // batch
    tt = min(tt, S)
    nt = S // tt
    kq = n_heads
    vq = 2 * n_heads * GLA_DK // GLA_DV
    rq = vq + n_heads
    kern = functools.partial(_gla_kernel, n_chunks=tt // GLA_CHUNK)
    rows = lambda off: (lambda b, h, t: (b * nt + t, off + h))
    return pl.pallas_call(
        kern,
        grid=(batch, n_heads, nt),
        in_specs=[pl.BlockSpec((tt, GLA_DK), rows(0)),
                  pl.BlockSpec((tt, GLA_DK), rows(kq)),
                  pl.BlockSpec((tt, GLA_DV), rows(vq)),
                  pl.BlockSpec((tt, GLA_DV), rows(rq)),
                  pl.BlockSpec((tt, LANE), lambda b, h, t: (b * nt + t, 0)),
                  pl.BlockSpec((LANE, GLA_DK), lambda b, h, t: (0, h)),
                  pl.BlockSpec((1, GLA_DK), lambda b, h, t: (0, h)),
                  pl.BlockSpec((1, GLA_DV), lambda b, h, t: (0, h))],
        out_specs=pl.BlockSpec((tt, GLA_DV), rows(0)),
        out_shape=jax.ShapeDtypeStruct((T, n_heads * GLA_DV), jnp.bfloat16),
        scratch_shapes=[pltpu.VMEM((GLA_DV, GLA_DK), jnp.float32)],
        compiler_params=_params(("parallel", "parallel", "arbitrary"), 48),
        name="gla_core",
    )(proj, proj, proj, proj, g_low, wlr, b_gate, g_norm)


def _mem_attn_kernel(x_ref, wq_ref, kv_ref, wo_ref, g_ref, b_ref, o_ref, ob_ref):
    x = x_ref[...]
    q = jnp.dot(x.astype(jnp.bfloat16), wq_ref[...],
                preferred_element_type=jnp.float32).astype(jnp.bfloat16)
    kv = kv_ref[...]
    width = MEM_HEADS * MEM_HEAD_DIM
    outs = []
    for hd in range(MEM_HEADS):
        cols = slice(hd * MEM_HEAD_DIM, (hd + 1) * MEM_HEAD_DIM)
        kh = kv[:, cols]
        vh = kv[:, width + hd * MEM_HEAD_DIM: width + (hd + 1) * MEM_HEAD_DIM]
        s = lax.dot_general(q[:, cols], kh, _NT,
                            preferred_element_type=jnp.float32) * (MEM_HEAD_DIM ** -0.5)
        s = s - jnp.max(s, axis=-1, keepdims=True)
        p = jnp.exp(s)
        p = p / jnp.sum(p, axis=-1, keepdims=True)
        outs.append(jnp.dot(p.astype(jnp.bfloat16), vh, preferred_element_type=jnp.float32))
    o = jnp.concatenate(outs, axis=-1).astype(jnp.bfloat16)
    hres = jnp.dot(o, wo_ref[...], preferred_element_type=jnp.float32)
    out = _layer_norm_rows(DEEPNORM_ALPHA * x + hres, g_ref[...], b_ref[...])
    o_ref[...] = out
    ob_ref[...] = out.astype(jnp.bfloat16)


def _mem_attn_ln(x, kv, wq, wo, g, b, batch, tm=256):
    T, D = x.shape
    S = T // batch
    tm = min(tm, S)
    ns = S // tm
    n_mem = kv.shape[0] // batch
    row = pl.BlockSpec((tm, D), lambda i: (i, 0))
    vec = pl.BlockSpec((1, D), lambda i: (0, 0))
    return pl.pallas_call(
        _mem_attn_kernel,
        grid=(T // tm,),
        in_specs=[row,
                  pl.BlockSpec(wq.shape, lambda i: (0, 0)),
                  pl.BlockSpec((n_mem, kv.shape[1]), lambda i: (i // ns, 0)),
                  pl.BlockSpec(wo.shape, lambda i: (0, 0)),
                  vec, vec],
        out_specs=[row, row],
        out_shape=[jax.ShapeDtypeStruct((T, D), jnp.float32),
                   jax.ShapeDtypeStruct((T, D), jnp.bfloat16)],
        compiler_params=_params(("parallel",), 56),
        name="mem_attn_ln",
    )(x, wq, kv, wo, g.reshape(1, D), b.reshape(1, D))


def _swiglu_kernel(te_ref, nv_ref, x_ref, wg_ref, wu_ref, wd_ref, o_ref):
    t = pl.program_id(0)
    j = pl.program_id(1)

    @pl.when(j == 0)
    def _():
        o_ref[...] = jnp.zeros_like(o_ref)

    @pl.when(t < nv_ref[0])
    def _():
        x = x_ref[...]
        g = jnp.dot(x, wg_ref[0], preferred_element_type=jnp.float32)
        u = jnp.dot(x, wu_ref[0], preferred_element_type=jnp.float32)
        a = (_silu(g) * u).astype(jnp.bfloat16)
        o_ref[...] += jnp.dot(a, wd_ref[0], preferred_element_type=jnp.float32)


def _grouped_swiglu(rows, w_gate_up, w_down, tile_e, n_valid, bm, tf=256):
    R, D = rows.shape
    F = w_down.shape[1]
    nf = F // tf
    assert R % bm == 0 and F % tf == 0

    def wmap(off):
        def index(t, j, te, nv):
            live = t < nv[0]
            tt = jnp.where(live, t, nv[0] - 1)
            jj = jnp.where(live, j, nf - 1)
            return te[tt], 0, off + jj
        return index

    def dmap(t, j, te, nv):
        live = t < nv[0]
        tt = jnp.where(live, t, nv[0] - 1)
        jj = jnp.where(live, j, nf - 1)
        return te[tt], jj, 0

    grid_spec = pltpu.PrefetchScalarGridSpec(
        num_scalar_prefetch=2,
        grid=(R // bm, nf),
        in_specs=[pl.BlockSpec((bm, D), lambda t, j, te, nv: (t, 0)),
                  pl.BlockSpec((1, D, tf), wmap(0)),
                  pl.BlockSpec((1, D, tf), wmap(nf)),
                  pl.BlockSpec((1, tf, D), dmap)],
        out_specs=pl.BlockSpec((bm, D), lambda t, j, te, nv: (t, 0)),
    )
    return pl.pallas_call(
        _swiglu_kernel,
        grid_spec=grid_spec,
        out_shape=jax.ShapeDtypeStruct((R, D), jnp.float32),
        compiler_params=_params(("parallel", "arbitrary"), 56),
        name="grouped_swiglu",
    )(tile_e, n_valid, rows, w_gate_up, w_gate_up, w_down)


def _gather_rows_kernel(src_ref, x_hbm, o_ref, buf_ref, sem):
    t = pl.program_id(0)
    bm = buf_ref.shape[0]

    def copy(r):
        return pltpu.make_async_copy(x_hbm.at[pl.ds(src_ref[t * bm + r], 1)],
                                     buf_ref.at[pl.ds(r, 1)], sem)

    def start(r, c):
        copy(r).start()
        return c

    def wait(r, c):
        copy(r).wait()
        return c

    lax.fori_loop(0, bm, start, 0)
    lax.fori_loop(0, bm, wait, 0)
    o_ref[...] = buf_ref[...].astype(o_ref.dtype)


def _gather_rows(x, src, bm=256):
    T, D = x.shape
    R = src.shape[0]
    grid_spec = pltpu.PrefetchScalarGridSpec(
        num_scalar_prefetch=1,
        grid=(R // bm,),
        in_specs=[pl.BlockSpec(memory_space=pl.ANY)],
        out_specs=pl.BlockSpec((bm, D), lambda t, s: (t, 0)),
        scratch_shapes=[pltpu.VMEM((bm, D), jnp.float32), pltpu.SemaphoreType.DMA(())],
    )
    return pl.pallas_call(
        _gather_rows_kernel,
        grid_spec=grid_spec,
        out_shape=jax.ShapeDtypeStruct((R, D), jnp.bfloat16),
        compiler_params=_params(("arbitrary",), 32),
        name="gather_rows",
    )(src, x)


def _combine_ln_kernel(p0_ref, p1_ref, y_hbm, x_ref, gate_ref, g_ref, b_ref, o_ref,
                       buf0_ref, buf1_ref, sem):
    t = pl.program_id(0)
    tm = buf0_ref.shape[0]

    def copies(r):
        tok = t * tm + r
        return (pltpu.make_async_copy(y_hbm.at[pl.ds(p0_ref[tok], 1)],
                                      buf0_ref.at[pl.ds(r, 1)], sem.at[0]),
                pltpu.make_async_copy(y_hbm.at[pl.ds(p1_ref[tok], 1)],
                                      buf1_ref.at[pl.ds(r, 1)], sem.at[1]))

    def start(r, c):
        a, b = copies(r)
        a.start()
        b.start()
        return c

    def wait(r, c):
        a, b = copies(r)
        a.wait()
        b.wait()
        return c

    lax.fori_loop(0, tm, start, 0)
    lax.fori_loop(0, tm, wait, 0)
    gates = gate_ref[...]
    hres = buf0_ref[...] * gates[:, 0:1] + buf1_ref[...] * gates[:, 1:2]
    o_ref[...] = _layer_norm_rows(DEEPNORM_ALPHA * x_ref[...] + hres, g_ref[...], b_ref[...])


def _combine_ln(x, y_rows, p0, p1, gates, g, b, tm=256):
    T, D = x.shape
    tm = min(tm, T)
    row = pl.BlockSpec((tm, D), lambda t, a, c: (t, 0))
    vec = pl.BlockSpec((1, D), lambda t, a, c: (0, 0))
    grid_spec = pltpu.PrefetchScalarGridSpec(
        num_scalar_prefetch=2,
        grid=(T // tm,),
        in_specs=[pl.BlockSpec(memory_space=pl.ANY), row,
                  pl.BlockSpec((tm, TOP_K), lambda t, a, c: (t, 0)), vec, vec],
        out_specs=row,
        scratch_shapes=[pltpu.VMEM((tm, D), jnp.float32), pltpu.VMEM((tm, D), jnp.float32),
                        pltpu.SemaphoreType.DMA((2,))],
    )
    return pl.pallas_call(
        _combine_ln_kernel,
        grid_spec=grid_spec,
        out_shape=jax.ShapeDtypeStruct((T, D), jnp.float32),
        compiler_params=_params(("arbitrary",), 48),
        name="combine_ln",
    )(p0, p1, y_rows, x, gates, g.reshape(1, D), b.reshape(1, D))


def _route(logits, n_experts, bm):
    T = logits.shape[0]
    top_logit, top_e = lax.top_k(logits, TOP_K)
    gates = jax.nn.softmax(top_logit, axis=-1)
    flat_e = top_e.reshape(-1)
    onehot = (flat_e[:, None] == jnp.arange(n_experts, dtype=flat_e.dtype)[None, :]).astype(jnp.int32)
    rank = jnp.sum((jnp.cumsum(onehot, axis=0) - onehot) * onehot, axis=1)
    counts = jnp.sum(onehot, axis=0)
    padded = (counts + bm - 1) // bm * bm
    pend = jnp.cumsum(padded)
    pstart = pend - padded
    dest = (pstart[flat_e] + rank).astype(jnp.int32)
    n_tiles = -(-(T * TOP_K) // bm) + n_experts
    tile_first = jnp.arange(n_tiles, dtype=jnp.int32) * bm
    tile_e = jnp.minimum(jnp.sum(tile_first[:, None] >= pend[None, :], axis=1),
                         n_experts - 1).astype(jnp.int32)
    n_valid = (pend[-1] // bm).astype(jnp.int32).reshape(1)
    src = jnp.zeros((n_tiles * bm,), jnp.int32).at[dest].set(
        jnp.repeat(jnp.arange(T, dtype=jnp.int32), TOP_K))
    pos = dest.reshape(T, TOP_K)
    return gates, pos[:, 0], pos[:, 1], src, tile_e, n_valid


def kernel(x, mem, l0_w_in, l0_b_forget, l0_w_out, l0_ln_mix_g, l0_ln_mix_b, l0_mem_wq, l0_mem_wkv, l0_mem_wo, l0_ln_mem_g, l0_ln_mem_b, l0_ffn_w_gate_up, l0_ffn_w_down, l0_ln_ffn_g, l0_ln_ffn_b, l1_w_in, l1_w_gate_lr, l1_b_gate, l1_g_head_norm, l1_w_out, l1_ln_mix_g, l1_ln_mix_b, l1_mem_wq, l1_mem_wkv, l1_mem_wo, l1_ln_mem_g, l1_ln_mem_b, l1_router, l1_moe_w_gate_up, l1_moe_w_down, l1_ln_ffn_g, l1_ln_ffn_b):
    bf16 = jnp.bfloat16
    B, S, D = x.shape
    T = B * S
    xf = x.reshape(T, D)
    xb = xf.astype(bf16)
    mem_b = mem.reshape(-1, D).astype(bf16)

    n_fox = l0_b_forget.shape[0]
    width = (l0_w_in.shape[1] - n_fox) // 3
    n_heads = width // HEAD_DIM
    n_sb = n_heads - n_fox
    qkv = _matmul(xb, l0_w_in[:, :3 * width].astype(bf16), bf16, name="l0_qkv_proj")
    f_prefix = _forget_prefix(xf, l0_w_in[:, 3 * width:].T, l0_b_forget, B)
    attn = _sb_fox_attention(qkv, f_prefix.reshape(B * n_fox, 1, S), B, n_heads, n_sb)
    hmix = _matmul(attn, l0_w_out.astype(bf16), jnp.float32, name="l0_out_proj")
    xf, xb = _add_ln(xf, hmix, l0_ln_mix_g, l0_ln_mix_b, name="l0_ln_mix")

    kv0 = _matmul(mem_b, l0_mem_wkv.astype(bf16), bf16, name="l0_mem_kv")
    xf, xb = _mem_attn_ln(xf, kv0, l0_mem_wq.astype(bf16), l0_mem_wo.astype(bf16),
                          l0_ln_mem_g, l0_ln_mem_b, B)

    bm_dense = min(512, T)
    dense_tiles = T // bm_dense
    hffn = _grouped_swiglu(xb, l0_ffn_w_gate_up.astype(bf16)[None], l0_ffn_w_down.astype(bf16)[None],
                           jnp.zeros((dense_tiles,), jnp.int32),
                           jnp.full((1,), dense_tiles, jnp.int32), bm_dense)
    xf, xb = _add_ln(xf, hffn, l0_ln_ffn_g, l0_ln_ffn_b, name="l0_ln_ffn")

    gla_heads = l1_w_gate_lr.shape[1] // GLA_DK
    rank = l1_w_gate_lr.shape[0]
    n_main = 2 * gla_heads * GLA_DK + 2 * gla_heads * GLA_DV
    proj = _matmul(xb, l1_w_in[:, :n_main].astype(bf16), bf16, name="l1_gla_proj")
    w_low = jnp.pad(l1_w_in[:, n_main:], ((0, 0), (0, LANE - rank)))
    g_low = _matmul_f32_narrow(xf, w_low, name="l1_gate_low")
    wlr = jnp.pad(l1_w_gate_lr, ((0, LANE - rank), (0, 0)))
    gla = _gla_core(proj, g_low, wlr, l1_b_gate.reshape(1, -1), l1_g_head_norm.reshape(1, -1),
                    B, gla_heads)
    hmix = _matmul(gla, l1_w_out.astype(bf16), jnp.float32, name="l1_out_proj")
    xf, xb = _add_ln(xf, hmix, l1_ln_mix_g, l1_ln_mix_b, name="l1_ln_mix")

    kv1 = _matmul(mem_b, l1_mem_wkv.astype(bf16), bf16, name="l1_mem_kv")
    xf, xb = _mem_attn_ln(xf, kv1, l1_mem_wq.astype(bf16), l1_mem_wo.astype(bf16),
                          l1_ln_mem_g, l1_ln_mem_b, B)

    n_experts = l1_router.shape[1]
    bm_moe = min(512, T)
    w_router = jnp.pad(l1_router, ((0, 0), (0, LANE - n_experts)))
    logits = _matmul_f32_narrow(xf, w_router, name="l1_router")[:, :n_experts]
    gates, p0, p1, src, tile_e, n_valid = _route(logits, n_experts, bm_moe)
    rows = _gather_rows(xf, src)
    y_rows = _grouped_swiglu(rows, l1_moe_w_gate_up.astype(bf16), l1_moe_w_down.astype(bf16),
                             tile_e, n_valid, bm_moe)
    out = _combine_ln(xf, y_rows, p0, p1, gates, l1_ln_ffn_g, l1_ln_ffn_b)
    return out.reshape(B, S, D)
```

```python
import functools

import jax
import jax.numpy as jnp
from jax import lax
from jax.experimental import pallas as pl
from jax.experimental.pallas import tpu as pltpu

HEAD_DIM = 128
MEM_HEADS = 4
MEM_HEAD_DIM = 128
GLA_DK = 256
GLA_DV = 512
GLA_CHUNK = 64
GLA_TAU = 16.0
TOP_K = 2
LN_EPS = 1e-5
RMS_EPS = 1e-6
DEPTH = 2
DEEPNORM_ALPHA = (2.0 * DEPTH) ** 0.25
NEG_BIG = -1e30
ATTN_BLK = 128
ATTN_WIN = 3
UNDERFLOW = 104.0
LANE = 128
MIB = 1024 * 1024

_HI = lax.Precision.HIGHEST
_NT = (((1,), (1,)), ((), ()))
_TN = (((0,), (0,)), ((), ()))


def _params(sem, vmem_mib):
    return pltpu.CompilerParams(dimension_semantics=sem, vmem_limit_bytes=vmem_mib * MIB)


def _softplus(z):
    return jnp.maximum(z, 0.0) + jnp.log(1.0 + jnp.exp(-jnp.abs(z)))


def _log_sigmoid(z):
    return jnp.minimum(z, 0.0) - jnp.log(1.0 + jnp.exp(-jnp.abs(z)))


def _silu(z):
    return z / (1.0 + jnp.exp(-z))


def _mm_kernel(x_ref, w_ref, o_ref):
    o_ref[...] = jnp.dot(x_ref[...], w_ref[...],
                         preferred_element_type=jnp.float32).astype(o_ref.dtype)


def _matmul(x, w, out_dtype, tm=1024, tn=1024, name="matmul"):
    M, K = x.shape
    N = w.shape[1]
    tm, tn = min(tm, M), min(tn, N)
    assert M % tm == 0 and N % tn == 0
    return pl.pallas_call(
        _mm_kernel,
        grid=(M // tm, N // tn),
        in_specs=[pl.BlockSpec((tm, K), lambda i, j: (i, 0)),
                  pl.BlockSpec((K, tn), lambda i, j: (0, j))],
        out_specs=pl.BlockSpec((tm, tn), lambda i, j: (i, j)),
        out_shape=jax.ShapeDtypeStruct((M, N), out_dtype),
        compiler_params=_params(("parallel", "parallel"), 56),
        name=name,
    )(x, w)


def _mm_hi_kernel(x_ref, w_ref, o_ref):
    o_ref[...] = jnp.dot(x_ref[...], w_ref[...], precision=_HI,
                         preferred_element_type=jnp.float32)


def _matmul_f32_narrow(x, w, tm=512, name="narrow_proj"):
    M, K = x.shape
    N = w.shape[1]
    tm = min(tm, M)
    assert M % tm == 0
    return pl.pallas_call(
        _mm_hi_kernel,
        grid=(M // tm,),
        in_specs=[pl.BlockSpec((tm, K), lambda i: (i, 0)),
                  pl.BlockSpec((K, N), lambda i: (0, 0))],
        out_specs=pl.BlockSpec((tm, N), lambda i: (i, 0)),
        out_shape=jax.ShapeDtypeStruct((M, N), jnp.float32),
        compiler_params=_params(("parallel",), 48),
        name=name,
    )(x, w)


def _layer_norm_rows(y, g, b):
    mu = jnp.mean(y, axis=-1, keepdims=True)
    yc = y - mu
    var = jnp.mean(yc * yc, axis=-1, keepdims=True)
    return yc * lax.rsqrt(var + LN_EPS) * g + b


def _add_ln_kernel(x_ref, h_ref, g_ref, b_ref, o_ref, ob_ref):
    y = DEEPNORM_ALPHA * x_ref[...] + h_ref[...].astype(jnp.float32)
    out = _layer_norm_rows(y, g_ref[...], b_ref[...])
    o_ref[...] = out
    ob_ref[...] = out.astype(jnp.bfloat16)


def _add_ln(x, h, g, b, tm=256, name="add_ln"):
    T, D = x.shape
    tm = min(tm, T)
    row = pl.BlockSpec((tm, D), lambda i: (i, 0))
    vec = pl.BlockSpec((1, D), lambda i: (0, 0))
    return pl.pallas_call(
        _add_ln_kernel,
        grid=(T // tm,),
        in_specs=[row, row, vec, vec],
        out_specs=[row, row],
        out_shape=[jax.ShapeDtypeStruct((T, D), jnp.float32),
                   jax.ShapeDtypeStruct((T, D), jnp.bfloat16)],
        compiler_params=_params(("parallel",), 48),
        name=name,
    )(x, h, g.reshape(1, D), b.reshape(1, D))


def _mm_add_ln_kernel(a_ref, w_ref, x_ref, g_ref, b_ref, o_ref, ob_ref):
    k = pl.program_id(1)

    @pl.when(k == 0)
    def _():
        o_ref[...] = jnp.zeros_like(o_ref)

    o_ref[...] += jnp.dot(a_ref[...], w_ref[...], preferred_element_type=jnp.float32)

    @pl.when(k == pl.num_programs(1) - 1)
    def _():
        out = _layer_norm_rows(DEEPNORM_ALPHA * x_ref[...] + o_ref[...], g_ref[...], b_ref[...])
        o_ref[...] = out
        ob_ref[...] = out.astype(jnp.bfloat16)


def _matmul_add_ln(a, w, x, g, b, tm=512, tk=512, name="matmul_add_ln"):
    T, K = a.shape
    D = w.shape[1]
    tm, tk = min(tm, T), min(tk, K)
    assert T % tm == 0 and K % tk == 0
    row = pl.BlockSpec((tm, D), lambda i, k: (i, 0), pipeline_mode=pl.Buffered(1))
    vec = pl.BlockSpec((1, D), lambda i, k: (0, 0))
    return pl.pallas_call(
        _mm_add_ln_kernel,
        grid=(T // tm, K // tk),
        in_specs=[pl.BlockSpec((tm, tk), lambda i, k: (i, k)),
                  pl.BlockSpec((tk, D), lambda i, k: (k, 0)),
                  row, vec, vec],
        out_specs=[row, row],
        out_shape=[jax.ShapeDtypeStruct((T, D), jnp.float32),
                   jax.ShapeDtypeStruct((T, D), jnp.bfloat16)],
        compiler_params=_params(("parallel", "arbitrary"), 56),
        name=name,
    )(a, w, x, g.reshape(1, D), b.reshape(1, D))


def _fgate_kernel(x_ref, wf_ref, b_ref, f_ref, carry_ref):
    s = pl.program_id(1)

    @pl.when(s == 0)
    def _():
        carry_ref[...] = jnp.zeros_like(carry_ref)

    ts = x_ref.shape[0]
    p = lax.dot_general(wf_ref[...], x_ref[...], _NT, precision=_HI,
                        preferred_element_type=jnp.float32)
    lf = _log_sigmoid(p + b_ref[...])
    r = lax.broadcasted_iota(jnp.int32, (ts, ts), 0)
    c = lax.broadcasted_iota(jnp.int32, (ts, ts), 1)
    upper = jnp.where(r <= c, 1.0, 0.0).astype(jnp.float32)
    cs = jnp.dot(lf, upper, precision=_HI, preferred_element_type=jnp.float32)
    cs = cs + carry_ref[:, 0:1]
    f_ref[0] = cs
    carry_ref[...] = jnp.broadcast_to(cs[:, ts - 1:ts], carry_ref.shape)


def _forget_prefix(x, wf_t, b_forget, batch, ts=512):
    T, D = x.shape
    S = T // batch
    Hf = wf_t.shape[0]
    ts = min(ts, S)
    ns = S // ts
    return pl.pallas_call(
        _fgate_kernel,
        grid=(batch, ns),
        in_specs=[pl.BlockSpec((ts, D), lambda b, s: (b * ns + s, 0)),
                  pl.BlockSpec((Hf, D), lambda b, s: (0, 0)),
                  pl.BlockSpec((Hf, 1), lambda b, s: (0, 0))],
        out_specs=pl.BlockSpec((1, Hf, ts), lambda b, s: (b, 0, s)),
        out_shape=jax.ShapeDtypeStruct((batch, Hf, S), jnp.float32),
        scratch_shapes=[pltpu.VMEM((Hf, LANE), jnp.float32)],
        compiler_params=_params(("parallel", "arbitrary"), 48),
        name="forget_prefix",
    )(x, wf_t, b_forget.reshape(Hf, 1))


def _attn_kernel(q_ref, k_ref, v_ref, f_ref, o_ref, kmax_ref, *, n_sb, nsub, scale):
    f32, bf16 = jnp.float32, jnp.bfloat16
    h = pl.program_id(1)
    i = pl.program_id(2)
    blk = ATTN_BLK
    win = ATTN_WIN * blk

    @pl.when(i == 0)
    def _key_norm_bound():
        S = k_ref.shape[0]
        chunk = min(4 * blk, S)

        def body(c, mx):
            kf = k_ref[pl.ds(pl.multiple_of(c * chunk, chunk), chunk), :].astype(f32)
            return jnp.maximum(mx, jnp.sum(kf * kf, axis=-1, keepdims=True))

        mx = lax.fori_loop(0, S // chunk, body, jnp.zeros((chunk, 1), f32))
        kmax_ref[...] = jnp.broadcast_to(jnp.max(mx, axis=0, keepdims=True), kmax_ref.shape)

    kmax = kmax_ref[0:1, 0:1]
    rel = (lax.broadcasted_iota(jnp.int32, (blk, win), 1)
           - lax.broadcasted_iota(jnp.int32, (blk, win), 0))
    rw = lax.broadcasted_iota(jnp.int32, (win, win), 0)
    cw = lax.broadcasted_iota(jnp.int32, (win, win), 1)
    lower_w = jnp.where(rw >= cw, 1.0, 0.0).astype(bf16)
    lower_b = lower_w[:blk, :blk]

    def sub_tile(r):
        R = i * nsub + r
        kb0 = jnp.maximum(R - (ATTN_WIN - 1), 0)
        ks = pl.multiple_of(kb0 * blk, blk)
        off = (R - kb0) * blk
        q = q_ref[r * blk:(r + 1) * blk, :]
        qf = q.astype(f32)
        zb = scale * jnp.sqrt(jnp.sum(qf * qf, axis=-1, keepdims=True) * kmax)
        kw = k_ref[pl.ds(ks, win), :]
        vw = v_ref[pl.ds(ks, win), :]
        z = lax.dot_general(q, kw, _NT, preferred_element_type=f32) * scale

        def key_block(jb):
            start = pl.multiple_of(jb * blk, blk)
            kb = k_ref[pl.ds(start, blk), :]
            return (lax.dot_general(q, kb, _NT, preferred_element_type=f32) * scale,
                    v_ref[pl.ds(start, blk), :])

        return R, kb0, ks, off, zb, vw, z, key_block

    def cumsum_rev(sp, lower):
        hi = sp.astype(bf16)
        lo = (sp - hi.astype(f32)).astype(bf16)
        return (jnp.dot(hi, lower, preferred_element_type=f32)
                + jnp.dot(lo, lower, preferred_element_type=f32))

    def loop_cond(st):
        return jnp.logical_and(st[0] >= 0, st[1] > 0)

    @pl.when(h < n_sb)
    def _stick_breaking():
        tiles = []
        for r in range(nsub):
            R, kb0, ks, off, zb, vw, z, key_block = sub_tile(r)
            before = rel < off
            sp = jnp.where(before, _softplus(z), 0.0)
            rem = cumsum_rev(sp, lower_w)
            w = jnp.where(before, jnp.exp(z - rem), 0.0)
            acc = jnp.dot(w.astype(bf16), vw, preferred_element_type=f32)
            tiles.append((kb0, zb, key_block, rem[:, 0:1], acc))

        for r, (kb0, zb, key_block, carry, acc) in enumerate(tiles):
            def live(carry, zb=zb):
                return (jnp.min(carry - zb) < UNDERFLOW).astype(jnp.int32)

            def body(st, key_block=key_block, live=live):
                jb, _, carry, acc = st
                zj, vb = key_block(jb)
                rem_local = cumsum_rev(_softplus(zj), lower_b)
                wj = jnp.exp(zj - (rem_local + carry))
                acc = acc + jnp.dot(wj.astype(bf16), vb, preferred_element_type=f32)
                carry = carry + rem_local[:, 0:1]
                return jb - 1, live(carry), carry, acc

            st = lax.while_loop(loop_cond, body, (kb0 - 1, live(carry), carry, acc))
            o_ref[r * blk:(r + 1) * blk, :] = st[3].astype(o_ref.dtype)

    @pl.when(h >= n_sb)
    def _forgetting():
        tiles = []
        for r in range(nsub):
            R, kb0, ks, off, zb, vw, z, key_block = sub_tile(r)
            f_first = f_ref[0, :, pl.ds(pl.multiple_of(R * blk, blk), blk)][:, 0:1]
            s = z + (f_first - f_ref[0, :, pl.ds(ks, win)])
            s = jnp.where(rel <= off, s, NEG_BIG)
            m = jnp.max(s, axis=-1, keepdims=True)
            p = jnp.exp(s - m)
            l = jnp.sum(p, axis=-1, keepdims=True)
            acc = jnp.dot(p.astype(bf16), vw, preferred_element_type=f32)
            tiles.append((kb0, zb, key_block, f_first, m, l, acc))

        for r, (kb0, zb, key_block, f_first, m, l, acc) in enumerate(tiles):
            def bias(jb, f_first=f_first):
                start = pl.multiple_of(jnp.maximum(jb, 0) * blk, blk)
                return f_first - f_ref[0, :, pl.ds(start, blk)]

            def live(jb, m, zb=zb, bias=bias):
                top = jnp.max(bias(jb), axis=-1, keepdims=True)
                return (jnp.max(zb + top - m) > -UNDERFLOW).astype(jnp.int32)

            def body(st, key_block=key_block, bias=bias, live=live):
                jb, _, m, l, acc = st
                zj, vb = key_block(jb)
                sj = zj + bias(jb)
                m_new = jnp.maximum(m, jnp.max(sj, axis=-1, keepdims=True))
                a = jnp.exp(m - m_new)
                pj = jnp.exp(sj - m_new)
                l = a * l + jnp.sum(pj, axis=-1, keepdims=True)
                acc = a * acc + jnp.dot(pj.astype(bf16), vb, preferred_element_type=f32)
                return jb - 1, live(jb - 1, m_new), m_new, l, acc

            st = lax.while_loop(loop_cond, body, (kb0 - 1, live(kb0 - 1, m), m, l, acc))
            o_ref[r * blk:(r + 1) * blk, :] = (st[4] / st[3]).astype(o_ref.dtype)


def _sb_fox_attention(qkv, f_prefix, batch, n_heads, n_sb, tq=512):
    T = qkv.shape[0]
    S = T // batch
    tq = min(tq, S)
    nq = S // tq
    n_fox = n_heads - n_sb
    assert S >= ATTN_WIN * ATTN_BLK and tq % ATTN_BLK == 0
    kern = functools.partial(_attn_kernel, n_sb=n_sb, nsub=tq // ATTN_BLK, scale=HEAD_DIM ** -0.5)
    return pl.pallas_call(
        kern,
        grid=(batch, n_heads, nq),
        in_specs=[pl.BlockSpec((tq, HEAD_DIM), lambda b, h, i: (b * nq + i, h)),
                  pl.BlockSpec((S, HEAD_DIM), lambda b, h, i: (b, n_heads + h)),
                  pl.BlockSpec((S, HEAD_DIM), lambda b, h, i: (b, 2 * n_heads + h)),
                  pl.BlockSpec((1, 1, S),
                               lambda b, h, i: (b * n_fox + jnp.maximum(h - n_sb, 0), 0, 0))],
        out_specs=pl.BlockSpec((tq, HEAD_DIM), lambda b, h, i: (b * nq + i, h)),
        out_shape=jax.ShapeDtypeStruct((T, n_heads * HEAD_DIM), jnp.bfloat16),
        scratch_shapes=[pltpu.VMEM((8, LANE), jnp.float32)],
        compiler_params=_params(("parallel", "parallel", "arbitrary"), 48),
        name="sb_fox_attention",
    )(qkv, qkv, qkv, f_prefix)


def _gla_kernel(q_ref, k_ref, v_ref, r_ref, gl_ref, wlr_ref, bg_ref, gn_ref, o_ref, state_ref,
                *, n_chunks):
    C = GLA_CHUNK

    @pl.when(pl.program_id(2) == 0)
    def _():
        state_ref[...] = jnp.zeros_like(state_ref)

    row = lax.broadcasted_iota(jnp.int32, (C, C), 0)
    col = lax.broadcasted_iota(jnp.int32, (C, C), 1)
    causal = row >= col
    tri = jnp.where(causal, 1.0, 0.0).astype(jnp.float32)
    wlr = wlr_ref[...]
    bg = bg_ref[...]
    gn = gn_ref[...]
    scale = GLA_DK ** -0.5

    for c in range(n_chunks):
        rows = pl.ds(c * C, C)
        gate_in = jnp.dot(gl_ref[rows, :], wlr, precision=_HI,
                          preferred_element_type=jnp.float32) + bg
        log_a = _log_sigmoid(gate_in) / GLA_TAU
        G = jnp.dot(tri, log_a, precision=_HI, preferred_element_type=jnp.float32)
        g_last = G[C - 1:C, :]
        qf = q_ref[rows, :].astype(jnp.float32) * scale
        kf = k_ref[rows, :].astype(jnp.float32)
        vb = v_ref[rows, :]
        q_dec = (qf * jnp.exp(G)).astype(jnp.bfloat16)
        k_inv = (kf * jnp.exp(-G)).astype(jnp.bfloat16)
        k_tail = (kf * jnp.exp(g_last - G)).astype(jnp.bfloat16)
        attn = lax.dot_general(q_dec, k_inv, _NT, preferred_element_type=jnp.float32)
        attn = jnp.where(causal, attn, 0.0).astype(jnp.bfloat16)
        o = jnp.dot(attn, vb, preferred_element_type=jnp.float32)
        state_t = state_ref[...]
        o = o + lax.dot_general(q_dec, state_t.astype(jnp.bfloat16), _NT,
                                preferred_element_type=jnp.float32)
        state_ref[...] = state_t * jnp.exp(g_last) + lax.dot_general(
            vb, k_tail, _TN, preferred_element_type=jnp.float32)
        o = o * lax.rsqrt(jnp.mean(o * o, axis=-1, keepdims=True) + RMS_EPS) * gn
        rf = r_ref[rows, :].astype(jnp.float32)
        o_ref[rows, :] = (o * _silu(rf)).astype(o_ref.dtype)


def _gla_core(proj, g_low, wlr, b_gate, g_norm, batch, n_heads, tt=256):
    T = proj.shape[0]
    S = T // batch
    tt = min(tt, S)
    nt = S // tt
    kq = n_heads
    vq = 2 * n_heads * GLA_DK // GLA_DV
    rq = vq + n_heads
    kern = functools.partial(_gla_kernel, n_chunks=tt // GLA_CHUNK)
    rows = lambda off: (lambda b, h, t: (b * nt + t, off + h))
    return pl.pallas_call(
        kern,
        grid=(batch, n_heads, nt),
        in_specs=[pl.BlockSpec((tt, GLA_DK), rows(0)),
                  pl.BlockSpec((tt, GLA_DK), rows(kq)),
                  pl.BlockSpec((tt, GLA_DV), rows(vq)),
                  pl.BlockSpec((tt, GLA_DV), rows(rq)),
                  pl.BlockSpec((tt, LANE), lambda b, h, t: (b * nt + t, 0)),
                  pl.BlockSpec((LANE, GLA_DK), lambda b, h, t: (0, h)),
                  pl.BlockSpec((1, GLA_DK), lambda b, h, t: (0, h)),
                  pl.BlockSpec((1, GLA_DV), lambda b, h, t: (0, h))],
        out_specs=pl.BlockSpec((tt, GLA_DV), rows(0)),
        out_shape=jax.ShapeDtypeStruct((T, n_heads * GLA_DV), jnp.bfloat16),
        scratch_shapes=[pltpu.VMEM((GLA_DV, GLA_DK), jnp.float32)],
        compiler_params=_params(("parallel", "parallel", "arbitrary"), 48),
        name="gla_core",
    )(proj, proj, proj, proj, g_low, wlr, b_gate, g_norm)


def _mem_attn_kernel(x_ref, wq_ref, kv_ref, wo_ref, g_ref, b_ref, o_ref, ob_ref):
    x = x_ref[...]
    q = jnp.dot(x.astype(jnp.bfloat16), wq_ref[...],
                preferred_element_type=jnp.float32).astype(jnp.bfloat16)
    kv = kv_ref[...]
    width = MEM_HEADS * MEM_HEAD_DIM
    outs = []
    for hd in range(MEM_HEADS):
        cols = slice(hd * MEM_HEAD_DIM, (hd + 1) * MEM_HEAD_DIM)
        kh = kv[:, cols]
        vh = kv[:, width + hd * MEM_HEAD_DIM: width + (hd + 1) * MEM_HEAD_DIM]
        s = lax.dot_general(q[:, cols], kh, _NT,
                            preferred_element_type=jnp.float32) * (MEM_HEAD_DIM ** -0.5)
        s = s - jnp.max(s, axis=-1, keepdims=True)
        p = jnp.exp(s)
        p = p / jnp.sum(p, axis=-1, keepdims=True)
        outs.append(jnp.dot(p.astype(jnp.bfloat16), vh, preferred_element_type=jnp.float32))
    o = jnp.concatenate(outs, axis=-1).astype(jnp.bfloat16)
    hres = jnp.dot(o, wo_ref[...], preferred_element_type=jnp.float32)
    out = _layer_norm_rows(DEEPNORM_ALPHA * x + hres, g_ref[...], b_ref[...])
    o_ref[...] = out
    ob_ref[...] = out.astype(jnp.bfloat16)


def _mem_attn_ln(x, kv, wq, wo, g, b, batch, tm=256):
    T, D = x.shape
    S = T // batch
    tm = min(tm, S)
    ns = S // tm
    n_mem = kv.shape[0] // batch
    row = pl.BlockSpec((tm, D), lambda i: (i, 0))
    vec = pl.BlockSpec((1, D), lambda i: (0, 0))
    return pl.pallas_call(
        _mem_attn_kernel,
        grid=(T // tm,),
        in_specs=[row,
                  pl.BlockSpec(wq.shape, lambda i: (0, 0)),
                  pl.BlockSpec((n_mem, kv.shape[1]), lambda i: (i // ns, 0)),
                  pl.BlockSpec(wo.shape, lambda i: (0, 0)),
                  vec, vec],
        out_specs=[row, row],
        out_shape=[jax.ShapeDtypeStruct((T, D), jnp.float32),
                   jax.ShapeDtypeStruct((T, D), jnp.bfloat16)],
        compiler_params=_params(("parallel",), 56),
        name="mem_attn_ln",
    )(x, wq, kv, wo, g.reshape(1, D), b.reshape(1, D))


def _swiglu_kernel(te_ref, nv_ref, x_ref, wg_ref, wu_ref, wd_ref, o_ref):
    t = pl.program_id(0)
    j = pl.program_id(1)

    @pl.when(j == 0)
    def _():
        o_ref[...] = jnp.zeros_like(o_ref)

    @pl.when(t < nv_ref[0])
    def _():
        x = x_ref[...]
        g = jnp.dot(x, wg_ref[0], preferred_element_type=jnp.float32)
        u = jnp.dot(x, wu_ref[0], preferred_element_type=jnp.float32)
        a = (_silu(g) * u).astype(jnp.bfloat16)
        o_ref[...] += jnp.dot(a, wd_ref[0], preferred_element_type=jnp.float32)


def _grouped_swiglu(rows, w_gate_up, w_down, tile_e, n_valid, bm, tf=256, single_buffer_rows=False):
    R, D = rows.shape
    F = w_down.shape[1]
    nf = F // tf
    assert R % bm == 0 and F % tf == 0
    row_mode = dict(pipeline_mode=pl.Buffered(1)) if single_buffer_rows else {}

    def wmap(off):
        def index(t, j, te, nv):
            live = t < nv[0]
            tt = jnp.where(live, t, nv[0] - 1)
            jj = jnp.where(live, j, nf - 1)
            return te[tt], 0, off + jj
        return index

    def dmap(t, j, te, nv):
        live = t < nv[0]
        tt = jnp.where(live, t, nv[0] - 1)
        jj = jnp.where(live, j, nf - 1)
        return te[tt], jj, 0

    grid_spec = pltpu.PrefetchScalarGridSpec(
        num_scalar_prefetch=2,
        grid=(R // bm, nf),
        in_specs=[pl.BlockSpec((bm, D), lambda t, j, te, nv: (t, 0), **row_mode),
                  pl.BlockSpec((1, D, tf), wmap(0)),
                  pl.BlockSpec((1, D, tf), wmap(nf)),
                  pl.BlockSpec((1, tf, D), dmap)],
        out_specs=pl.BlockSpec((bm, D), lambda t, j, te, nv: (t, 0), **row_mode),
    )
    return pl.pallas_call(
        _swiglu_kernel,
        grid_spec=grid_spec,
        out_shape=jax.ShapeDtypeStruct((R, D), jnp.float32),
        compiler_params=_params(("parallel", "arbitrary"), 56),
        name="grouped_swiglu",
    )(tile_e, n_valid, rows, w_gate_up, w_gate_up, w_down)


def _gather_rows_kernel(src_ref, x_hbm, o_ref, buf_ref, sem):
    t = pl.program_id(0)
    bm = buf_ref.shape[1]
    slot = t % 2

    def copy(tile, slot, r):
        return pltpu.make_async_copy(x_hbm.at[pl.ds(src_ref[tile * bm + r], 1)],
                                     buf_ref.at[slot, pl.ds(r, 1)], sem.at[slot])

    def start_tile(tile, slot):
        def start(r, c):
            copy(tile, slot, r).start()
            return c
        lax.fori_loop(0, bm, start, 0, unroll=8)

    @pl.when(t == 0)
    def _():
        start_tile(0, 0)

    @pl.when(t + 1 < pl.num_programs(0))
    def _():
        start_tile(t + 1, 1 - slot)

    def wait(r, c):
        copy(t, slot, r).wait()
        return c

    lax.fori_loop(0, bm, wait, 0, unroll=8)
    o_ref[...] = buf_ref[slot].astype(o_ref.dtype)


def _gather_rows(x, src, bm=256):
    T, D = x.shape
    R = src.shape[0]
    grid_spec = pltpu.PrefetchScalarGridSpec(
        num_scalar_prefetch=1,
        grid=(R // bm,),
        in_specs=[pl.BlockSpec(memory_space=pl.ANY)],
        out_specs=pl.BlockSpec((bm, D), lambda t, s: (t, 0)),
        scratch_shapes=[pltpu.VMEM((2, bm, D), jnp.float32), pltpu.SemaphoreType.DMA((2,))],
    )
    return pl.pallas_call(
        _gather_rows_kernel,
        grid_spec=grid_spec,
        out_shape=jax.ShapeDtypeStruct((R, D), jnp.bfloat16),
        compiler_params=_params(("arbitrary",), 32),
        name="gather_rows",
    )(src, x)


def _combine_ln_kernel(p0_ref, p1_ref, y_hbm, x_ref, gate_ref, g_ref, b_ref, o_ref,
                       buf0_ref, buf1_ref, sem):
    t = pl.program_id(0)
    tm = buf0_ref.shape[1]
    slot = t % 2

    def copies(tile, slot, r):
        tok = tile * tm + r
        return (pltpu.make_async_copy(y_hbm.at[pl.ds(p0_ref[tok], 1)],
                                      buf0_ref.at[slot, pl.ds(r, 1)], sem.at[0, slot]),
                pltpu.make_async_copy(y_hbm.at[pl.ds(p1_ref[tok], 1)],
                                      buf1_ref.at[slot, pl.ds(r, 1)], sem.at[1, slot]))

    def start_tile(tile, slot):
        def start(r, c):
            a, b = copies(tile, slot, r)
            a.start()
            b.start()
            return c
        lax.fori_loop(0, tm, start, 0, unroll=8)

    @pl.when(t == 0)
    def _():
        start_tile(0, 0)

    @pl.when(t + 1 < pl.num_programs(0))
    def _():
        start_tile(t + 1, 1 - slot)

    def wait(r, c):
        a, b = copies(t, slot, r)
        a.wait()
        b.wait()
        return c

    lax.fori_loop(0, tm, wait, 0, unroll=8)
    gates = gate_ref[...]
    hres = buf0_ref[slot] * gates[:, 0:1] + buf1_ref[slot] * gates[:, 1:2]
    o_ref[...] = _layer_norm_rows(DEEPNORM_ALPHA * x_ref[...] + hres, g_ref[...], b_ref[...])


def _combine_ln(x, y_rows, p0, p1, gates, g, b, tm=256):
    T, D = x.shape
    tm = min(tm, T)
    row = pl.BlockSpec((tm, D), lambda t, a, c: (t, 0))
    vec = pl.BlockSpec((1, D), lambda t, a, c: (0, 0))
    grid_spec = pltpu.PrefetchScalarGridSpec(
        num_scalar_prefetch=2,
        grid=(T // tm,),
        in_specs=[pl.BlockSpec(memory_space=pl.ANY), row,
                  pl.BlockSpec((tm, TOP_K), lambda t, a, c: (t, 0)), vec, vec],
        out_specs=row,
        scratch_shapes=[pltpu.VMEM((2, tm, D), jnp.float32), pltpu.VMEM((2, tm, D), jnp.float32),
                        pltpu.SemaphoreType.DMA((2, 2))],
    )
    return pl.pallas_call(
        _combine_ln_kernel,
        grid_spec=grid_spec,
        out_shape=jax.ShapeDtypeStruct((T, D), jnp.float32),
        compiler_params=_params(("arbitrary",), 48),
        name="combine_ln",
    )(p0, p1, y_rows, x, gates, g.reshape(1, D), b.reshape(1, D))


def _route(logits, n_experts, bm):
    T = logits.shape[0]
    top_logit, top_e = lax.top_k(logits, TOP_K)
    gates = jax.nn.softmax(top_logit, axis=-1)
    flat_e = top_e.reshape(-1)
    onehot = (flat_e[:, None] == jnp.arange(n_experts, dtype=flat_e.dtype)[None, :]).astype(jnp.int32)
    rank = jnp.sum((jnp.cumsum(onehot, axis=0) - onehot) * onehot, axis=1)
    counts = jnp.sum(onehot, axis=0)
    padded = (counts + bm - 1) // bm * bm
    pend = jnp.cumsum(padded)
    pstart = pend - padded
    dest = (pstart[flat_e] + rank).astype(jnp.int32)
    n_tiles = -(-(T * TOP_K) // bm) + n_experts
    tile_first = jnp.arange(n_tiles, dtype=jnp.int32) * bm
    tile_e = jnp.minimum(jnp.sum(tile_first[:, None] >= pend[None, :], axis=1),
                         n_experts - 1).astype(jnp.int32)
    n_valid = (pend[-1] // bm).astype(jnp.int32).reshape(1)
    src = jnp.zeros((n_tiles * bm,), jnp.int32).at[dest].set(
        jnp.repeat(jnp.arange(T, dtype=jnp.int32), TOP_K))
    pos = dest.reshape(T, TOP_K)
    return gates, pos[:, 0], pos[:, 1], src, tile_e, n_valid


def kernel(x, mem, l0_w_in, l0_b_forget, l0_w_out, l0_ln_mix_g, l0_ln_mix_b, l0_mem_wq, l0_mem_wkv, l0_mem_wo, l0_ln_mem_g, l0_ln_mem_b, l0_ffn_w_gate_up, l0_ffn_w_down, l0_ln_ffn_g, l0_ln_ffn_b, l1_w_in, l1_w_gate_lr, l1_b_gate, l1_g_head_norm, l1_w_out, l1_ln_mix_g, l1_ln_mix_b, l1_mem_wq, l1_mem_wkv, l1_mem_wo, l1_ln_mem_g, l1_ln_mem_b, l1_router, l1_moe_w_gate_up, l1_moe_w_down, l1_ln_ffn_g, l1_ln_ffn_b):
    bf16 = jnp.bfloat16
    B, S, D = x.shape
    T = B * S
    xf = x.reshape(T, D)
    xb = xf.astype(bf16)
    mem_b = mem.reshape(-1, D).astype(bf16)

    n_fox = l0_b_forget.shape[0]
    width = (l0_w_in.shape[1] - n_fox) // 3
    n_heads = width // HEAD_DIM
    n_sb = n_heads - n_fox
    qkv = _matmul(xb, l0_w_in[:, :3 * width].astype(bf16), bf16, name="l0_qkv_proj")
    f_prefix = _forget_prefix(xf, l0_w_in[:, 3 * width:].T, l0_b_forget, B)
    attn = _sb_fox_attention(qkv, f_prefix.reshape(B * n_fox, 1, S), B, n_heads, n_sb)
    xf, xb = _matmul_add_ln(attn, l0_w_out.astype(bf16), xf, l0_ln_mix_g, l0_ln_mix_b,
                            name="l0_out_proj_ln")

    kv0 = _matmul(mem_b, l0_mem_wkv.astype(bf16), bf16, name="l0_mem_kv")
    xf, xb = _mem_attn_ln(xf, kv0, l0_mem_wq.astype(bf16), l0_mem_wo.astype(bf16),
                          l0_ln_mem_g, l0_ln_mem_b, B)

    bm_dense = min(1024, T)
    dense_tiles = T // bm_dense
    hffn = _grouped_swiglu(xb, l0_ffn_w_gate_up.astype(bf16)[None], l0_ffn_w_down.astype(bf16)[None],
                           jnp.zeros((dense_tiles,), jnp.int32),
                           jnp.full((1,), dense_tiles, jnp.int32), bm_dense,
                           single_buffer_rows=True)
    xf, xb = _add_ln(xf, hffn, l0_ln_ffn_g, l0_ln_ffn_b, name="l0_ln_ffn")

    gla_heads = l1_w_gate_lr.shape[1] // GLA_DK
    rank = l1_w_gate_lr.shape[0]
    n_main = 2 * gla_heads * GLA_DK + 2 * gla_heads * GLA_DV
    proj = _matmul(xb, l1_w_in[:, :n_main].astype(bf16), bf16, name="l1_gla_proj")
    w_low = jnp.pad(l1_w_in[:, n_main:], ((0, 0), (0, LANE - rank)))
    g_low = _matmul_f32_narrow(xf, w_low, name="l1_gate_low")
    wlr = jnp.pad(l1_w_gate_lr, ((0, LANE - rank), (0, 0)))
    gla = _gla_core(proj, g_low, wlr, l1_b_gate.reshape(1, -1), l1_g_head_norm.reshape(1, -1),
                    B, gla_heads)
    xf, xb = _matmul_add_ln(gla, l1_w_out.astype(bf16), xf, l1_ln_mix_g, l1_ln_mix_b,
                            name="l1_out_proj_ln")

    kv1 = _matmul(mem_b, l1_mem_wkv.astype(bf16), bf16, name="l1_mem_kv")
    xf, xb = _mem_attn_ln(xf, kv1, l1_mem_wq.astype(bf16), l1_mem_wo.astype(bf16),
                          l1_ln_mem_g, l1_ln_mem_b, B)

    n_experts = l1_router.shape[1]
    bm_moe = min(512, T)
    w_router = jnp.pad(l1_router, ((0, 0), (0, LANE - n_experts)))
    logits = _matmul_f32_narrow(xf, w_router, name="l1_router")[:, :n_experts]
    gates, p0, p1, src, tile_e, n_valid = _route(logits, n_experts, bm_moe)
    rows = _gather_rows(xf, src)
    f_expert = l1_moe_w_down.shape[1]
    y_rows = _grouped_swiglu(rows, l1_moe_w_gate_up.astype(bf16), l1_moe_w_down.astype(bf16),
                             tile_e, n_valid, bm_moe, tf=512 if f_expert % 512 == 0 else 256)
    out = _combine_ln(xf, y_rows, p0, p1, gates, l1_ln_ffn_g, l1_ln_ffn_b)
    return out.reshape(B, S, D)
```

```python
import functools

import jax
import jax.numpy as jnp
from jax import lax
from jax.experimental import pallas as pl
from jax.experimental.pallas import tpu as pltpu

HEAD_DIM = 128
MEM_HEADS = 4
MEM_HEAD_DIM = 128
GLA_DK = 256
GLA_DV = 512
GLA_CHUNK = 64
GLA_TAU = 16.0
TOP_K = 2
LN_EPS = 1e-5
RMS_EPS = 1e-6
DEPTH = 2
DEEPNORM_ALPHA = (2.0 * DEPTH) ** 0.25
NEG_BIG = -1e30
ATTN_BLK = 256
ATTN_WIN = 2
UNDERFLOW = 104.0
LANE = 128
MIB = 1024 * 1024

_HI = lax.Precision.HIGHEST
_NT = (((1,), (1,)), ((), ()))
_TN = (((0,), (0,)), ((), ()))


def _params(sem, vmem_mib):
    return pltpu.CompilerParams(dimension_semantics=sem, vmem_limit_bytes=vmem_mib * MIB)


def _softplus(z):
    return jnp.maximum(z, 0.0) + jnp.log(1.0 + jnp.exp(-jnp.abs(z)))


def _log_sigmoid(z):
    return jnp.minimum(z, 0.0) - jnp.log(1.0 + jnp.exp(-jnp.abs(z)))


def _silu(z):
    return z / (1.0 + jnp.exp(-z))


def _split_bf16(x, terms):
    pieces = []
    for _ in range(terms):
        p = x.astype(jnp.bfloat16)
        pieces.append(p)
        x = x - p.astype(jnp.float32)
    return pieces


def _mm_kernel(x_ref, w_ref, o_ref):
    o_ref[...] = jnp.dot(x_ref[...], w_ref[...],
                         preferred_element_type=jnp.float32).astype(o_ref.dtype)


def _matmul(x, w, out_dtype, n_cols=None, tm=1024, tn=1024, name="matmul"):
    M, K = x.shape
    N = w.shape[1] if n_cols is None else n_cols
    tm, tn = min(tm, M), min(tn, N)
    assert M % tm == 0 and N % tn == 0
    return pl.pallas_call(
        _mm_kernel,
        grid=(M // tm, N // tn),
        in_specs=[pl.BlockSpec((tm, K), lambda i, j: (i, 0)),
                  pl.BlockSpec((K, tn), lambda i, j: (0, j))],
        out_specs=pl.BlockSpec((tm, tn), lambda i, j: (i, j)),
        out_shape=jax.ShapeDtypeStruct((M, N), out_dtype),
        compiler_params=_params(("parallel", "parallel"), 56),
        name=name,
    )(x, w)


def _mm_hi_kernel(x_ref, w_ref, o_ref):
    o_ref[...] = jnp.dot(x_ref[...], w_ref[...], precision=_HI,
                         preferred_element_type=jnp.float32)


def _matmul_f32_narrow(x, w, tm=512, name="narrow_proj"):
    M, K = x.shape
    N = w.shape[1]
    tm = min(tm, M)
    assert M % tm == 0
    return pl.pallas_call(
        _mm_hi_kernel,
        grid=(M // tm,),
        in_specs=[pl.BlockSpec((tm, K), lambda i: (i, 0)),
                  pl.BlockSpec((K, N), lambda i: (0, 0))],
        out_specs=pl.BlockSpec((tm, N), lambda i: (i, 0)),
        out_shape=jax.ShapeDtypeStruct((M, N), jnp.float32),
        compiler_params=_params(("parallel",), 48),
        name=name,
    )(x, w)


def _layer_norm_rows(y, g, b):
    mu = jnp.mean(y, axis=-1, keepdims=True)
    yc = y - mu
    var = jnp.mean(yc * yc, axis=-1, keepdims=True)
    return yc * lax.rsqrt(var + LN_EPS) * g + b


def _add_ln_kernel(x_ref, h_ref, g_ref, b_ref, o_ref, ob_ref):
    y = DEEPNORM_ALPHA * x_ref[...] + h_ref[...].astype(jnp.float32)
    out = _layer_norm_rows(y, g_ref[...], b_ref[...])
    o_ref[...] = out
    ob_ref[...] = out.astype(jnp.bfloat16)


def _add_ln(x, h, g, b, tm=256, name="add_ln"):
    T, D = x.shape
    tm = min(tm, T)
    row = pl.BlockSpec((tm, D), lambda i: (i, 0))
    vec = pl.BlockSpec((1, D), lambda i: (0, 0))
    return pl.pallas_call(
        _add_ln_kernel,
        grid=(T // tm,),
        in_specs=[row, row, vec, vec],
        out_specs=[row, row],
        out_shape=[jax.ShapeDtypeStruct((T, D), jnp.float32),
                   jax.ShapeDtypeStruct((T, D), jnp.bfloat16)],
        compiler_params=_params(("parallel",), 48),
        name=name,
    )(x, h, g.reshape(1, D), b.reshape(1, D))


def _fgate_kernel(x_ref, wf_ref, b_ref, f_ref, carry_ref):
    s = pl.program_id(1)

    @pl.when(s == 0)
    def _():
        carry_ref[...] = jnp.zeros_like(carry_ref)

    ts = x_ref.shape[0]
    p = lax.dot_general(wf_ref[...], x_ref[...], _NT, precision=_HI,
                        preferred_element_type=jnp.float32)
    lf = _log_sigmoid(p + b_ref[...])
    r = lax.broadcasted_iota(jnp.int32, (ts, ts), 0)
    c = lax.broadcasted_iota(jnp.int32, (ts, ts), 1)
    upper = jnp.where(r <= c, 1.0, 0.0).astype(jnp.float32)
    cs = jnp.dot(lf, upper, precision=_HI, preferred_element_type=jnp.float32)
    cs = cs + carry_ref[:, 0:1]
    f_ref[0] = cs
    carry_ref[...] = jnp.broadcast_to(cs[:, ts - 1:ts], carry_ref.shape)


def _forget_prefix(x, wf_t, b_forget, batch, ts=512):
    T, D = x.shape
    S = T // batch
    Hf = wf_t.shape[0]
    ts = min(ts, S)
    ns = S // ts
    return pl.pallas_call(
        _fgate_kernel,
        grid=(batch, ns),
        in_specs=[pl.BlockSpec((ts, D), lambda b, s: (b * ns + s, 0)),
                  pl.BlockSpec((Hf, D), lambda b, s: (0, 0)),
                  pl.BlockSpec((Hf, 1), lambda b, s: (0, 0))],
        out_specs=pl.BlockSpec((1, Hf, ts), lambda b, s: (b, 0, s)),
        out_shape=jax.ShapeDtypeStruct((batch, Hf, S), jnp.float32),
        scratch_shapes=[pltpu.VMEM((Hf, LANE), jnp.float32)],
        compiler_params=_params(("parallel", "arbitrary"), 48),
        name="forget_prefix",
    )(x, wf_t, b_forget.reshape(Hf, 1))


def _attn_kernel(q_ref, k_ref, v_ref, f_ref, o_ref, kmax_ref, *, n_sb, nsub, scale):
    f32, bf16 = jnp.float32, jnp.bfloat16
    h = pl.program_id(1)
    i = pl.program_id(2)
    blk = ATTN_BLK
    win = ATTN_WIN * blk

    @pl.when(i == 0)
    def _key_norm_bound():
        S = k_ref.shape[0]
        chunk = min(4 * blk, S)

        def body(c, mx):
            kf = k_ref[pl.ds(pl.multiple_of(c * chunk, chunk), chunk), :].astype(f32)
            return jnp.maximum(mx, jnp.sum(kf * kf, axis=-1, keepdims=True))

        mx = lax.fori_loop(0, S // chunk, body, jnp.zeros((chunk, 1), f32))
        kmax_ref[...] = jnp.broadcast_to(jnp.max(mx, axis=0, keepdims=True), kmax_ref.shape)

    kmax = kmax_ref[0:1, 0:1]
    rel = (lax.broadcasted_iota(jnp.int32, (blk, win), 1)
           - lax.broadcasted_iota(jnp.int32, (blk, win), 0))
    rw = lax.broadcasted_iota(jnp.int32, (win, win), 0)
    cw = lax.broadcasted_iota(jnp.int32, (win, win), 1)
    lower_w = jnp.where(rw >= cw, 1.0, 0.0).astype(bf16)
    lower_b = lower_w[:blk, :blk]

    def sub_tile(r):
        R = i * nsub + r
        kb0 = jnp.maximum(R - (ATTN_WIN - 1), 0)
        ks = pl.multiple_of(kb0 * blk, blk)
        off = (R - kb0) * blk
        q = q_ref[r * blk:(r + 1) * blk, :]
        qf = q.astype(f32)
        zb = scale * jnp.sqrt(jnp.sum(qf * qf, axis=-1, keepdims=True) * kmax)
        kw = k_ref[pl.ds(ks, win), :]
        vw = v_ref[pl.ds(ks, win), :]
        z = lax.dot_general(q, kw, _NT, preferred_element_type=f32) * scale

        def key_block(jb):
            start = pl.multiple_of(jb * blk, blk)
            kb = k_ref[pl.ds(start, blk), :]
            return (lax.dot_general(q, kb, _NT, preferred_element_type=f32) * scale,
                    v_ref[pl.ds(start, blk), :])

        return R, kb0, ks, off, zb, vw, z, key_block

    def cumsum_rev(sp, lower):
        hi = sp.astype(bf16)
        lo = (sp - hi.astype(f32)).astype(bf16)
        return (jnp.dot(hi, lower, preferred_element_type=f32)
                + jnp.dot(lo, lower, preferred_element_type=f32))

    def loop_cond(st):
        return jnp.logical_and(st[0] >= 0, st[1] > 0)

    @pl.when(h < n_sb)
    def _stick_breaking():
        tiles = []
        for r in range(nsub):
            R, kb0, ks, off, zb, vw, z, key_block = sub_tile(r)
            before = rel < off
            sp = jnp.where(before, _softplus(z), 0.0)
            rem = cumsum_rev(sp, lower_w)
            w = jnp.where(before, jnp.exp(z - rem), 0.0)
            acc = jnp.dot(w.astype(bf16), vw, preferred_element_type=f32)
            tiles.append((kb0, zb, key_block, rem[:, 0:1], acc))

        for r, (kb0, zb, key_block, carry, acc) in enumerate(tiles):
            def live(carry, zb=zb):
                return (jnp.min(carry - zb) < UNDERFLOW).astype(jnp.int32)

            def body(st, key_block=key_block, live=live):
                jb, _, carry, acc = st
                zj, vb = key_block(jb)
                rem_local = cumsum_rev(_softplus(zj), lower_b)
                wj = jnp.exp(zj - (rem_local + carry))
                acc = acc + jnp.dot(wj.astype(bf16), vb, preferred_element_type=f32)
                carry = carry + rem_local[:, 0:1]
                return jb - 1, live(carry), carry, acc

            st = lax.while_loop(loop_cond, body, (kb0 - 1, live(carry), carry, acc))
            o_ref[r * blk:(r + 1) * blk, :] = st[3].astype(o_ref.dtype)

    @pl.when(h >= n_sb)
    def _forgetting():
        tiles = []
        for r in range(nsub):
            R, kb0, ks, off, zb, vw, z, key_block = sub_tile(r)
            f_first = f_ref[0, :, pl.ds(pl.multiple_of(R * blk, blk), blk)][:, 0:1]
            s = z + (f_first - f_ref[0, :, pl.ds(ks, win)])
            s = jnp.where(rel <= off, s, NEG_BIG)
            m = jnp.max(s, axis=-1, keepdims=True)
            p = jnp.exp(s - m)
            l = jnp.sum(p, axis=-1, keepdims=True)
            acc = jnp.dot(p.astype(bf16), vw, preferred_element_type=f32)
            tiles.append((kb0, zb, key_block, f_first, m, l, acc))

        for r, (kb0, zb, key_block, f_first, m, l, acc) in enumerate(tiles):
            def bias(jb, f_first=f_first):
                start = pl.multiple_of(jnp.maximum(jb, 0) * blk, blk)
                return f_first - f_ref[0, :, pl.ds(start, blk)]

            def live(jb, m, zb=zb, bias=bias):
                top = jnp.max(bias(jb), axis=-1, keepdims=True)
                return (jnp.max(zb + top - m) > -UNDERFLOW).astype(jnp.int32)

            def body(st, key_block=key_block, bias=bias, live=live):
                jb, _, m, l, acc = st
                zj, vb = key_block(jb)
                sj = zj + bias(jb)
                m_new = jnp.maximum(m, jnp.max(sj, axis=-1, keepdims=True))
                a = jnp.exp(m - m_new)
                pj = jnp.exp(sj - m_new)
                l = a * l + jnp.sum(pj, axis=-1, keepdims=True)
                acc = a * acc + jnp.dot(pj.astype(bf16), vb, preferred_element_type=f32)
                return jb - 1, live(jb - 1, m_new), m_new, l, acc

            st = lax.while_loop(loop_cond, body, (kb0 - 1, live(kb0 - 1, m), m, l, acc))
            o_ref[r * blk:(r + 1) * blk, :] = (st[4] / st[3]).astype(o_ref.dtype)


def _sb_fox_attention(qkv, f_prefix, batch, n_heads, n_sb, tq=512):
    T = qkv.shape[0]
    S = T // batch
    tq = min(tq, S)
    nq = S // tq
    n_fox = n_heads - n_sb
    assert S >= ATTN_WIN * ATTN_BLK and tq % ATTN_BLK == 0
    kern = functools.partial(_attn_kernel, n_sb=n_sb, nsub=tq // ATTN_BLK, scale=HEAD_DIM ** -0.5)
    return pl.pallas_call(
        kern,
        grid=(batch, n_heads, nq),
        in_specs=[pl.BlockSpec((tq, HEAD_DIM), lambda b, h, i: (b * nq + i, h)),
                  pl.BlockSpec((S, HEAD_DIM), lambda b, h, i: (b, n_heads + h)),
                  pl.BlockSpec((S, HEAD_DIM), lambda b, h, i: (b, 2 * n_heads + h)),
                  pl.BlockSpec((1, 1, S),
                               lambda b, h, i: (b * n_fox + jnp.maximum(h - n_sb, 0), 0, 0))],
        out_specs=pl.BlockSpec((tq, HEAD_DIM), lambda b, h, i: (b * nq + i, h)),
        out_shape=jax.ShapeDtypeStruct((T, n_heads * HEAD_DIM), jnp.bfloat16),
        scratch_shapes=[pltpu.VMEM((8, LANE), jnp.float32)],
        compiler_params=_params(("parallel", "parallel", "arbitrary"), 48),
        name="sb_fox_attention",
    )(qkv, qkv, qkv, f_prefix)


def _gla_kernel(q_ref, k_ref, v_ref, r_ref, gl_ref, wlr_ref, bg_ref, gn_ref, o_ref, state_ref,
                *, n_chunks):
    C = GLA_CHUNK

    @pl.when(pl.program_id(2) == 0)
    def _():
        state_ref[...] = jnp.zeros_like(state_ref)

    tt = n_chunks * C
    row = lax.broadcasted_iota(jnp.int32, (tt, tt), 0)
    col = lax.broadcasted_iota(jnp.int32, (tt, tt), 1)
    chunk_start = row - jnp.bitwise_and(row, C - 1)
    tri = jnp.where(col <= row, jnp.where(col >= chunk_start, 1.0, 0.0), 0.0).astype(jnp.float32)
    causal = tri > 0.5
    gn = gn_ref[...]
    scale = GLA_DK ** -0.5

    def bdot(a, b):
        return jnp.dot(a, b, preferred_element_type=jnp.float32)

    g_hi, g_lo = _split_bf16(gl_ref[...], 2)
    w_hi, w_lo = _split_bf16(wlr_ref[...], 2)
    gate_in = bdot(g_hi, w_hi) + bdot(g_hi, w_lo) + bdot(g_lo, w_hi) + bg_ref[...]
    log_a = _log_sigmoid(gate_in) / GLA_TAU
    tri_b = tri.astype(jnp.bfloat16)
    G = sum(bdot(tri_b, piece) for piece in _split_bf16(log_a, 3))
    qf = q_ref[...].astype(jnp.float32) * scale
    kf = k_ref[...].astype(jnp.float32)
    vb = v_ref[...]
    q_dec = (qf * jnp.exp(G)).astype(jnp.bfloat16)
    k_inv = (kf * jnp.exp(-G)).astype(jnp.bfloat16)
    attn = lax.dot_general(q_dec, k_inv, _NT, preferred_element_type=jnp.float32)
    attn = jnp.where(causal, attn, 0.0).astype(jnp.bfloat16)
    o_intra = jnp.dot(attn, vb, preferred_element_type=jnp.float32)

    state_t = state_ref[...]
    for c in range(n_chunks):
        rows = slice(c * C, (c + 1) * C)
        g_last = G[(c + 1) * C - 1:(c + 1) * C, :]
        k_tail = (kf[rows] * jnp.exp(g_last - G[rows])).astype(jnp.bfloat16)
        o = o_intra[rows] + lax.dot_general(q_dec[rows], state_t.astype(jnp.bfloat16), _NT,
                                            preferred_element_type=jnp.float32)
        state_t = state_t * jnp.exp(g_last) + lax.dot_general(
            vb[rows], k_tail, _TN, preferred_element_type=jnp.float32)
        o = o * lax.rsqrt(jnp.mean(o * o, axis=-1, keepdims=True) + RMS_EPS) * gn
        rf = r_ref[rows, :].astype(jnp.float32)
        o_ref[rows, :] = (o * _silu(rf)).astype(o_ref.dtype)
    state_ref[...] = state_t


def _gla_core(proj, g_low, wlr, b_gate, g_norm, batch, n_heads, tt=256):
    T = proj.shape[0]
    S = T // batch
    tt = min(tt, S)
    nt = S // tt
    kq = n_heads
    vq = 2 * n_heads * GLA_DK // GLA_DV
    rq = vq + n_heads
    kern = functools.partial(_gla_kernel, n_chunks=tt // GLA_CHUNK)
    rows = lambda off: (lambda b, h, t: (b * nt + t, off + h))
    return pl.pallas_call(
        kern,
        grid=(batch, n_heads, nt),
        in_specs=[pl.BlockSpec((tt, GLA_DK), rows(0)),
                  pl.BlockSpec((tt, GLA_DK), rows(kq)),
                  pl.BlockSpec((tt, GLA_DV), rows(vq)),
                  pl.BlockSpec((tt, GLA_DV), rows(rq)),
                  pl.BlockSpec((tt, LANE), lambda b, h, t: (b * nt + t, 0)),
                  pl.BlockSpec((LANE, GLA_DK), lambda b, h, t: (0, h)),
                  pl.BlockSpec((1, GLA_DK), lambda b, h, t: (0, h)),
                  pl.BlockSpec((1, GLA_DV), lambda b, h, t: (0, h))],
        out_specs=pl.BlockSpec((tt, GLA_DV), rows(0)),
        out_shape=jax.ShapeDtypeStruct((T, n_heads * GLA_DV), jnp.bfloat16),
        scratch_shapes=[pltpu.VMEM((GLA_DV, GLA_DK), jnp.float32)],
        compiler_params=_params(("parallel", "parallel", "arbitrary"), 48),
        name="gla_core",
    )(proj, proj, proj, proj, g_low, wlr, b_gate, g_norm)


def _mem_attn_kernel(x_ref, wq_ref, kv_ref, wo_ref, g_ref, b_ref, o_ref, ob_ref):
    x = x_ref[...]
    q = jnp.dot(x.astype(jnp.bfloat16), wq_ref[...],
                preferred_element_type=jnp.float32).astype(jnp.bfloat16)
    kv = kv_ref[...]
    width = MEM_HEADS * MEM_HEAD_DIM
    outs = []
    for hd in range(MEM_HEADS):
        cols = slice(hd * MEM_HEAD_DIM, (hd + 1) * MEM_HEAD_DIM)
        kh = kv[:, cols]
        vh = kv[:, width + hd * MEM_HEAD_DIM: width + (hd + 1) * MEM_HEAD_DIM]
        s = lax.dot_general(q[:, cols], kh, _NT,
                            preferred_element_type=jnp.float32) * (MEM_HEAD_DIM ** -0.5)
        s = s - jnp.max(s, axis=-1, keepdims=True)
        p = jnp.exp(s)
        p = p / jnp.sum(p, axis=-1, keepdims=True)
        outs.append(jnp.dot(p.astype(jnp.bfloat16), vh, preferred_element_type=jnp.float32))
    o = jnp.concatenate(outs, axis=-1).astype(jnp.bfloat16)
    hres = jnp.dot(o, wo_ref[...], preferred_element_type=jnp.float32)
    out = _layer_norm_rows(DEEPNORM_ALPHA * x + hres, g_ref[...], b_ref[...])
    o_ref[...] = out
    ob_ref[...] = out.astype(jnp.bfloat16)


def _mem_attn_ln(x, kv, wq, wo, g, b, batch, tm=256):
    T, D = x.shape
    S = T // batch
    tm = min(tm, S)
    ns = S // tm
    n_mem = kv.shape[0] // batch
    row = pl.BlockSpec((tm, D), lambda i: (i, 0))
    vec = pl.BlockSpec((1, D), lambda i: (0, 0))
    return pl.pallas_call(
        _mem_attn_kernel,
        grid=(T // tm,),
        in_specs=[row,
                  pl.BlockSpec(wq.shape, lambda i: (0, 0)),
                  pl.BlockSpec((n_mem, kv.shape[1]), lambda i: (i // ns, 0)),
                  pl.BlockSpec(wo.shape, lambda i: (0, 0)),
                  vec, vec],
        out_specs=[row, row],
        out_shape=[jax.ShapeDtypeStruct((T, D), jnp.float32),
                   jax.ShapeDtypeStruct((T, D), jnp.bfloat16)],
        compiler_params=_params(("parallel",), 56),
        name="mem_attn_ln",
    )(x, wq, kv, wo, g.reshape(1, D), b.reshape(1, D))


def _swiglu_kernel(te_ref, nv_ref, x_ref, wg_ref, wu_ref, wd_ref, o_ref):
    t = pl.program_id(0)
    j = pl.program_id(1)

    @pl.when(j == 0)
    def _():
        o_ref[...] = jnp.zeros_like(o_ref)

    @pl.when(t < nv_ref[0])
    def _():
        x = x_ref[...]
        g = jnp.dot(x, wg_ref[0], preferred_element_type=jnp.float32)
        u = jnp.dot(x, wu_ref[0], preferred_element_type=jnp.float32)
        a = (_silu(g) * u).astype(jnp.bfloat16)
        o_ref[...] += jnp.dot(a, wd_ref[0], preferred_element_type=jnp.float32)


def _grouped_swiglu(rows, w_gate_up, w_down, tile_e, n_valid, bm, tf=256, single_buffer_rows=False):
    R, D = rows.shape
    F = w_down.shape[1]
    nf = F // tf
    assert R % bm == 0 and F % tf == 0
    row_mode = dict(pipeline_mode=pl.Buffered(1)) if single_buffer_rows else {}

    def wmap(off):
        def index(t, j, te, nv):
            live = t < nv[0]
            tt = jnp.where(live, t, nv[0] - 1)
            jj = jnp.where(live, j, nf - 1)
            return te[tt], 0, off + jj
        return index

    def dmap(t, j, te, nv):
        live = t < nv[0]
        tt = jnp.where(live, t, nv[0] - 1)
        jj = jnp.where(live, j, nf - 1)
        return te[tt], jj, 0

    grid_spec = pltpu.PrefetchScalarGridSpec(
        num_scalar_prefetch=2,
        grid=(R // bm, nf),
        in_specs=[pl.BlockSpec((bm, D), lambda t, j, te, nv: (t, 0), **row_mode),
                  pl.BlockSpec((1, D, tf), wmap(0)),
                  pl.BlockSpec((1, D, tf), wmap(nf)),
                  pl.BlockSpec((1, tf, D), dmap)],
        out_specs=pl.BlockSpec((bm, D), lambda t, j, te, nv: (t, 0), **row_mode),
    )
    return pl.pallas_call(
        _swiglu_kernel,
        grid_spec=grid_spec,
        out_shape=jax.ShapeDtypeStruct((R, D), jnp.float32),
        compiler_params=_params(("parallel", "arbitrary"), 56),
        name="grouped_swiglu",
    )(tile_e, n_valid, rows, w_gate_up, w_gate_up, w_down)


def _gather_rows_kernel(src_ref, x_hbm, o_ref, buf_ref, sem):
    t = pl.program_id(0)
    bm = buf_ref.shape[1]
    slot = t % 2

    def copy(tile, slot, r):
        return pltpu.make_async_copy(x_hbm.at[pl.ds(src_ref[tile * bm + r], 1)],
                                     buf_ref.at[slot, pl.ds(r, 1)], sem.at[slot])

    def start_tile(tile, slot):
        def start(r, c):
            copy(tile, slot, r).start()
            return c
        lax.fori_loop(0, bm, start, 0, unroll=8)

    @pl.when(t == 0)
    def _():
        start_tile(0, 0)

    @pl.when(t + 1 < pl.num_programs(0))
    def _():
        start_tile(t + 1, 1 - slot)

    def wait(r, c):
        copy(t, slot, r).wait()
        return c

    lax.fori_loop(0, bm, wait, 0, unroll=8)
    o_ref[...] = buf_ref[slot].astype(o_ref.dtype)


def _gather_rows(x, src, bm=256):
    T, D = x.shape
    R = src.shape[0]
    grid_spec = pltpu.PrefetchScalarGridSpec(
        num_scalar_prefetch=1,
        grid=(R // bm,),
        in_specs=[pl.BlockSpec(memory_space=pl.ANY)],
        out_specs=pl.BlockSpec((bm, D), lambda t, s: (t, 0)),
        scratch_shapes=[pltpu.VMEM((2, bm, D), jnp.float32), pltpu.SemaphoreType.DMA((2,))],
    )
    return pl.pallas_call(
        _gather_rows_kernel,
        grid_spec=grid_spec,
        out_shape=jax.ShapeDtypeStruct((R, D), jnp.bfloat16),
        compiler_params=_params(("arbitrary",), 32),
        name="gather_rows",
    )(src, x)


def _combine_ln_kernel(p0_ref, p1_ref, y_hbm, x_ref, gate_ref, g_ref, b_ref, o_ref,
                       buf0_ref, buf1_ref, sem):
    t = pl.program_id(0)
    tm = buf0_ref.shape[1]
    slot = t % 2

    def copies(tile, slot, r):
        tok = tile * tm + r
        return (pltpu.make_async_copy(y_hbm.at[pl.ds(p0_ref[tok], 1)],
                                      buf0_ref.at[slot, pl.ds(r, 1)], sem.at[0, slot]),
                pltpu.make_async_copy(y_hbm.at[pl.ds(p1_ref[tok], 1)],
                                      buf1_ref.at[slot, pl.ds(r, 1)], sem.at[1, slot]))

    def start_tile(tile, slot):
        def start(r, c):
            a, b = copies(tile, slot, r)
            a.start()
            b.start()
            return c
        lax.fori_loop(0, tm, start, 0, unroll=8)

    @pl.when(t == 0)
    def _():
        start_tile(0, 0)

    @pl.when(t + 1 < pl.num_programs(0))
    def _():
        start_tile(t + 1, 1 - slot)

    def wait(r, c):
        a, b = copies(t, slot, r)
        a.wait()
        b.wait()
        return c

    lax.fori_loop(0, tm, wait, 0, unroll=8)
    gates = gate_ref[...]
    hres = buf0_ref[slot] * gates[:, 0:1] + buf1_ref[slot] * gates[:, 1:2]
    o_ref[...] = _layer_norm_rows(DEEPNORM_ALPHA * x_ref[...] + hres, g_ref[...], b_ref[...])


def _combine_ln(x, y_rows, p0, p1, gates, g, b, tm=256):
    T, D = x.shape
    tm = min(tm, T)
    row = pl.BlockSpec((tm, D), lambda t, a, c: (t, 0))
    vec = pl.BlockSpec((1, D), lambda t, a, c: (0, 0))
    grid_spec = pltpu.PrefetchScalarGridSpec(
        num_scalar_prefetch=2,
        grid=(T // tm,),
        in_specs=[pl.BlockSpec(memory_space=pl.ANY), row,
                  pl.BlockSpec((tm, TOP_K), lambda t, a, c: (t, 0)), vec, vec],
        out_specs=row,
        scratch_shapes=[pltpu.VMEM((2, tm, D), jnp.float32), pltpu.VMEM((2, tm, D), jnp.float32),
                        pltpu.SemaphoreType.DMA((2, 2))],
    )
    return pl.pallas_call(
        _combine_ln_kernel,
        grid_spec=grid_spec,
        out_shape=jax.ShapeDtypeStruct((T, D), jnp.float32),
        compiler_params=_params(("arbitrary",), 48),
        name="combine_ln",
    )(p0, p1, y_rows, x, gates, g.reshape(1, D), b.reshape(1, D))


def _route(logits, n_experts, bm):
    T = logits.shape[0]
    top_logit, top_e = lax.top_k(logits, TOP_K)
    gates = jax.nn.softmax(top_logit, axis=-1)
    flat_e = top_e.reshape(-1)
    onehot = (flat_e[:, None] == jnp.arange(n_experts, dtype=flat_e.dtype)[None, :]).astype(jnp.int32)
    rank = jnp.sum((jnp.cumsum(onehot, axis=0) - onehot) * onehot, axis=1)
    counts = jnp.sum(onehot, axis=0)
    padded = (counts + bm - 1) // bm * bm
    pend = jnp.cumsum(padded)
    pstart = pend - padded
    dest = (pstart[flat_e] + rank).astype(jnp.int32)
    n_tiles = -(-(T * TOP_K) // bm) + n_experts
    tile_first = jnp.arange(n_tiles, dtype=jnp.int32) * bm
    tile_e = jnp.minimum(jnp.sum(tile_first[:, None] >= pend[None, :], axis=1),
                         n_experts - 1).astype(jnp.int32)
    n_valid = (pend[-1] // bm).astype(jnp.int32).reshape(1)
    src = jnp.zeros((n_tiles * bm,), jnp.int32).at[dest].set(
        jnp.repeat(jnp.arange(T, dtype=jnp.int32), TOP_K))
    pos = dest.reshape(T, TOP_K)
    return gates, pos[:, 0], pos[:, 1], src, tile_e, n_valid


def kernel(x, mem, l0_w_in, l0_b_forget, l0_w_out, l0_ln_mix_g, l0_ln_mix_b, l0_mem_wq, l0_mem_wkv, l0_mem_wo, l0_ln_mem_g, l0_ln_mem_b, l0_ffn_w_gate_up, l0_ffn_w_down, l0_ln_ffn_g, l0_ln_ffn_b, l1_w_in, l1_w_gate_lr, l1_b_gate, l1_g_head_norm, l1_w_out, l1_ln_mix_g, l1_ln_mix_b, l1_mem_wq, l1_mem_wkv, l1_mem_wo, l1_ln_mem_g, l1_ln_mem_b, l1_router, l1_moe_w_gate_up, l1_moe_w_down, l1_ln_ffn_g, l1_ln_ffn_b):
    bf16 = jnp.bfloat16
    B, S, D = x.shape
    T = B * S
    xf = x.reshape(T, D)
    xb = xf.astype(bf16)
    mem_b = mem.reshape(-1, D).astype(bf16)

    n_fox = l0_b_forget.shape[0]
    width = (l0_w_in.shape[1] - n_fox) // 3
    n_heads = width // HEAD_DIM
    n_sb = n_heads - n_fox
    qkv = _matmul(xb, l0_w_in.astype(bf16), bf16, n_cols=3 * width, name="l0_qkv_proj")
    f_prefix = _forget_prefix(xf, l0_w_in[:, 3 * width:].T, l0_b_forget, B)
    attn = _sb_fox_attention(qkv, f_prefix.reshape(B * n_fox, 1, S), B, n_heads, n_sb)
    hmix = _matmul(attn, l0_w_out.astype(bf16), bf16, name="l0_out_proj")
    xf, xb = _add_ln(xf, hmix, l0_ln_mix_g, l0_ln_mix_b, name="l0_ln_mix")

    kv0 = _matmul(mem_b, l0_mem_wkv.astype(bf16), bf16, name="l0_mem_kv")
    xf, xb = _mem_attn_ln(xf, kv0, l0_mem_wq.astype(bf16), l0_mem_wo.astype(bf16),
                          l0_ln_mem_g, l0_ln_mem_b, B)

    bm_dense = min(1024, T)
    dense_tiles = T // bm_dense
    hffn = _grouped_swiglu(xb, l0_ffn_w_gate_up.astype(bf16)[None], l0_ffn_w_down.astype(bf16)[None],
                           jnp.zeros((dense_tiles,), jnp.int32),
                           jnp.full((1,), dense_tiles, jnp.int32), bm_dense,
                           single_buffer_rows=True)
    xf, xb = _add_ln(xf, hffn, l0_ln_ffn_g, l0_ln_ffn_b, name="l0_ln_ffn")

    gla_heads = l1_w_gate_lr.shape[1] // GLA_DK
    rank = l1_w_gate_lr.shape[0]
    n_main = 2 * gla_heads * GLA_DK + 2 * gla_heads * GLA_DV
    proj = _matmul(xb, l1_w_in.astype(bf16), bf16, n_cols=n_main, name="l1_gla_proj")
    w_low = jnp.pad(l1_w_in[:, n_main:], ((0, 0), (0, LANE - rank)))
    g_low = _matmul_f32_narrow(xf, w_low, name="l1_gate_low")
    wlr = jnp.pad(l1_w_gate_lr, ((0, LANE - rank), (0, 0)))
    gla = _gla_core(proj, g_low, wlr, l1_b_gate.reshape(1, -1), l1_g_head_norm.reshape(1, -1),
                    B, gla_heads)
    hmix = _matmul(gla, l1_w_out.astype(bf16), bf16, name="l1_out_proj")
    xf, xb = _add_ln(xf, hmix, l1_ln_mix_g, l1_ln_mix_b, name="l1_ln_mix")

    kv1 = _matmul(mem_b, l1_mem_wkv.astype(bf16), bf16, name="l1_mem_kv")
    xf, xb = _mem_attn_ln(xf, kv1, l1_mem_wq.astype(bf16), l1_mem_wo.astype(bf16),
                          l1_ln_mem_g, l1_ln_mem_b, B)

    n_experts = l1_router.shape[1]
    bm_moe = min(512, T)
    w_router = jnp.pad(l1_router, ((0, 0), (0, LANE - n_experts)))
    logits = _matmul_f32_narrow(xf, w_router, name="l1_router")[:, :n_experts]
    gates, p0, p1, src, tile_e, n_valid = _route(logits, n_experts, bm_moe)
    rows = _gather_rows(xf, src)
    f_expert = l1_moe_w_down.shape[1]
    y_rows = _grouped_swiglu(rows, l1_moe_w_gate_up.astype(bf16), l1_moe_w_down.astype(bf16),
                             tile_e, n_valid, bm_moe, tf=512 if f_expert % 512 == 0 else 256)
    out = _combine_ln(xf, y_rows, p0, p1, gates, l1_ln_ffn_g, l1_ln_ffn_b)
    return out.reshape(B, S, D)
```
